```python
import math
import jax
import jax.numpy as jnp
from jax import lax
import numpy as np


D_MODEL = 1024
BATCH = 8
SEQ = 2048
DEPTH = 4

PLE_DIM = 256
N_EVEN = (DEPTH + 1) // 2
N_ODD = DEPTH // 2
EPS = 1e-6
NEG = -1e30

S5_WIDTH = D_MODEL // 2
S5_GROUP = 16
S5_GROUPS = S5_WIDTH // S5_GROUP
S5_STATE = 64
CONV_WIDTH = D_MODEL // 2
CONV_K = 31
NSA_HEADS = D_MODEL // 128
NSA_KV_HEADS = 2
NSA_GQA = NSA_HEADS // NSA_KV_HEADS
NSA_HEAD_DIM = 64
NSA_WIDTH = NSA_HEADS * NSA_HEAD_DIM
CMP_LEN = 32
CMP_STRIDE = 16
CMP_HIDDEN = 256
SEL_LEN = 64
SEL_TOPN = 8
WINDOW = 512
Q_BLOCK = 128
MLA_HEADS = D_MODEL // 128
MLA_NOPE = 64
MLA_ROPE = 32
MLA_V = 64
MLA_Q_RANK = 256
MLA_KV_RANK = 128
MLA_WIDTH = MLA_HEADS * MLA_V
ROPE_BASE = 10000.0

EVEN_SPLITS = (S5_WIDTH, S5_WIDTH, 2 * CONV_WIDTH, CONV_WIDTH)
ODD_SPLITS = (NSA_WIDTH, 6 * NSA_KV_HEADS * NSA_HEAD_DIM, 3 * NSA_HEADS, NSA_WIDTH,
              MLA_Q_RANK, MLA_KV_RANK, MLA_ROPE, MLA_WIDTH)
EVEN_IN = sum(EVEN_SPLITS)
ODD_IN = sum(ODD_SPLITS)
EVEN_MIX = S5_WIDTH + CONV_WIDTH
ODD_MIX = NSA_WIDTH + MLA_WIDTH

kernel_name = 'hybrid_s5conv_nsamla_trunk'


def rms_norm(x, g):
    xf = x.astype(jnp.float32)
    y = xf * lax.rsqrt(jnp.mean(xf * xf, axis=-1, keepdims=True) + EPS)
    return (y * g.astype(jnp.float32)).astype(x.dtype)


def layer_norm(x, g, b):
    xf = x.astype(jnp.float32)
    mu = jnp.mean(xf, axis=-1, keepdims=True)
    var = jnp.mean(jnp.square(xf - mu), axis=-1, keepdims=True)
    return ((xf - mu) * lax.rsqrt(var + EPS) * g + b).astype(x.dtype)


def split_cols(h, sizes):
    return jnp.split(h, np.cumsum(sizes)[:-1].tolist(), axis=-1)


def masked_softmax(s, mask):
    s = jnp.where(mask, s.astype(jnp.float32), NEG)
    m = jnp.max(s, axis=-1, keepdims=True)
    e = jnp.exp(s - m) * mask
    return e / jnp.maximum(jnp.sum(e, axis=-1, keepdims=True), 1e-30)


def alibi_slopes(n):
    return np.array([2.0 ** (-8.0 * (i + 1) / n) for i in range(n)], np.float32)


def rope(x, positions):
    half = MLA_ROPE // 2
    freqs = ROPE_BASE ** (-jnp.arange(half, dtype=jnp.float32) / half)
    ang = positions.astype(jnp.float32)[..., None] * freqs
    cos = jnp.cos(ang)[:, :, None, :]
    sin = jnp.sin(ang)[:, :, None, :]
    x1, x2 = x[..., :half], x[..., half:]
    return jnp.concatenate([x1 * cos - x2 * sin, x1 * sin + x2 * cos], axis=-1).astype(x.dtype)


def s5_mixer(u, lam_re, lam_im, b_re, b_im, c_re, c_im, d_skip, log_step, w_glu, b_glu):
    f32 = jnp.float32
    bsz, s_len, _ = u.shape
    step = jnp.exp(log_step.astype(f32))[:, None]
    lr, li = lam_re.astype(f32), lam_im.astype(f32)
    mag = jnp.exp(lr * step)
    ab_re, ab_im = mag * jnp.cos(li * step), mag * jnp.sin(li * step)
    den = lr * lr + li * li
    nr, ni = ab_re - 1.0, ab_im
    f_re, f_im = (nr * lr + ni * li) / den, (ni * lr - nr * li) / den
    br, bi = b_re.astype(f32), b_im.astype(f32)
    bb_re = f_re[..., None] * br - f_im[..., None] * bi
    bb_im = f_re[..., None] * bi + f_im[..., None] * br
    ut = jnp.swapaxes(u.astype(f32).reshape(bsz, s_len, S5_GROUPS, S5_GROUP), 0, 1)
    bu_re = jnp.einsum('sbgc,gnc->sbgn', ut, bb_re)
    bu_im = jnp.einsum('sbgc,gnc->sbgn', ut, bb_im)
    a_re = jnp.broadcast_to(ab_re[None, None], (s_len, 1, S5_GROUPS, S5_STATE))
    a_im = jnp.broadcast_to(ab_im[None, None], (s_len, 1, S5_GROUPS, S5_STATE))

    def combine(e1, e2):
        a1r, a1i, b1r, b1i = e1
        a2r, a2i, b2r, b2i = e2
        return (a2r * a1r - a2i * a1i, a2r * a1i + a2i * a1r,
                a2r * b1r - a2i * b1i + b2r, a2r * b1i + a2i * b1r + b2i)

    _, _, x_re, x_im = lax.associative_scan(combine, (a_re, a_im, bu_re, bu_im), axis=0)
    y = (jnp.einsum('sbgn,gcn->sbgc', x_re, c_re.astype(f32))
         - jnp.einsum('sbgn,gcn->sbgc', x_im, c_im.astype(f32)))
    y = jnp.swapaxes(y, 0, 1).reshape(bsz, s_len, S5_WIDTH) + d_skip * u
    y = jax.nn.gelu(y)
    return (y * jax.nn.sigmoid(y @ w_glu + b_glu)).astype(u.dtype)


def conv_mixer(v, w_dw, b_dw, ln_g, ln_b, w_pw):
    a, g = jnp.split(v, 2, axis=-1)
    h = a * jax.nn.sigmoid(g)
    h = lax.conv_general_dilated(h, w_dw[:, None, :].astype(h.dtype), window_strides=(1,),
                                 padding=[(CONV_K - 1, 0)],
                                 dimension_numbers=('NWC', 'WIO', 'NWC'),
                                 feature_group_count=CONV_WIDTH) + b_dw
    h = jax.nn.silu(layer_norm(h, ln_g, ln_b))
    return h @ w_pw


def even_mixer(h, w_in, lam_re, lam_im, b_re, b_im, c_re, c_im, d_skip, log_step, w_glu, b_glu,
               w_dw, b_dw, ln_g, ln_b, w_pw, w_out):
    u_a, g_a, v_b, g_b = split_cols(h @ w_in, EVEN_SPLITS)
    y_a = s5_mixer(u_a, lam_re, lam_im, b_re, b_im, c_re, c_im, d_skip, log_step, w_glu, b_glu) * jax.nn.silu(g_a)
    y_b = conv_mixer(v_b, w_dw, b_dw, ln_g, ln_b, w_pw) * jax.nn.silu(g_b)
    return jnp.concatenate([y_a, y_b], axis=-1) @ w_out


def compress_blocks(k, cmp_idx, pos, w1, w2):
    bsz, nc = k.shape[0], cmp_idx.shape[0]
    blk = k[:, cmp_idx] + pos[:, None, :]
    blk = jnp.moveaxis(blk, 3, 2).reshape(bsz, nc, NSA_KV_HEADS, CMP_LEN * NSA_HEAD_DIM)
    return jax.nn.gelu(blk @ w1) @ w2


def nsa_mixer(q, kv, gate_logits, cmp_pos_k, cmp_pos_v, ck_w1, ck_w2, cv_w1, cv_w2):
    f32 = jnp.float32
    bsz, s_len = q.shape[:2]
    hkv, grp, dh = NSA_KV_HEADS, NSA_GQA, NSA_HEAD_DIM
    k_cmp, v_cmp, k_sel, v_sel, k_win, v_win = [kv[:, :, i] for i in range(6)]
    qg = q.reshape(bsz, s_len, hkv, grp, dh)
    scale = dh ** -0.5
    slopes = jnp.asarray(alibi_slopes(NSA_HEADS)).reshape(hkv, grp)
    t = np.arange(s_len)

    nc = (s_len - CMP_LEN) // CMP_STRIDE + 1
    cmp_idx = np.arange(nc)[:, None] * CMP_STRIDE + np.arange(CMP_LEN)[None]
    kc = compress_blocks(k_cmp, cmp_idx, cmp_pos_k, ck_w1, ck_w2)
    vc = compress_blocks(v_cmp, cmp_idx, cmp_pos_v, cv_w1, cv_w2)
    dist_c = (t[:, None] - cmp_idx[:, -1][None]).astype(np.float32)
    s_c = jnp.einsum('bsygd,bcyd->bygsc', qg, kc).astype(f32) * scale - slopes[:, :, None, None] * dist_c
    p_c = masked_softmax(s_c, jnp.asarray(dist_c >= 0))
    o_cmp = jnp.einsum('bygsc,bcyd->bsygd', p_c, vc)

    ns = s_len // SEL_LEN
    n_top = min(SEL_TOPN, ns)
    sb = np.arange(ns)
    cs = np.arange(nc)[:, None] * CMP_STRIDE
    overlap = ((cs < (sb[None] + 1) * SEL_LEN) & (cs + CMP_LEN > sb[None] * SEL_LEN)).astype(np.float32)
    p_slc = jnp.einsum('bygsc,cj->bysj', p_c, jnp.asarray(overlap))
    cur = t[:, None] // SEL_LEN
    forced = (sb[None] == 0) | (sb[None] == cur) | (sb[None] == cur - 1)
    causal_blk = sb[None] * SEL_LEN <= t[:, None]
    sel_score = jnp.where(forced, p_slc + 1e4, jnp.where(causal_blk, p_slc, -1e4))
    _, sel_idx = lax.top_k(sel_score, n_top)

    k_blk = jnp.moveaxis(k_sel.reshape(bsz, ns, SEL_LEN, hkv, dh), 3, 1)
    v_blk = jnp.moveaxis(v_sel.reshape(bsz, ns, SEL_LEN, hkv, dh), 3, 1)
    k_pad = jnp.pad(k_win, ((0, 0), (WINDOW, 0), (0, 0), (0, 0)))
    v_pad = jnp.pad(v_win, ((0, 0), (WINDOW, 0), (0, 0), (0, 0)))
    bi = jnp.arange(bsz)[:, None, None, None]
    hi = jnp.arange(hkv)[None, :, None, None]
    span = WINDOW + Q_BLOCK
    n_sel_keys = n_top * SEL_LEN

    def one_block(i):
        t0 = i * Q_BLOCK
        tq = t0 + jnp.arange(Q_BLOCK)
        qb = lax.dynamic_slice_in_dim(qg, t0, Q_BLOCK, axis=1)
        ib = lax.dynamic_slice_in_dim(sel_idx, t0, Q_BLOCK, axis=2)
        ks = k_blk[bi, hi, ib].reshape(bsz, hkv, Q_BLOCK, n_sel_keys, dh)
        vs = v_blk[bi, hi, ib].reshape(bsz, hkv, Q_BLOCK, n_sel_keys, dh)
        kpos = (ib[..., None] * SEL_LEN + jnp.arange(SEL_LEN)).reshape(bsz, hkv, Q_BLOCK, n_sel_keys)
        dist = (tq[None, None, :, None] - kpos).astype(f32)[:, :, None]
        s_s = jnp.einsum('bqygd,byqkd->bygqk', qb, ks).astype(f32) * scale - slopes[None, :, :, None, None] * dist
        p_s = masked_softmax(s_s, dist >= 0)
        o_sel = jnp.einsum('bygqk,byqkd->bqygd', p_s, vs)
        kw = lax.dynamic_slice_in_dim(k_pad, t0, span, axis=1)
        vw = lax.dynamic_slice_in_dim(v_pad, t0, span, axis=1)
        kpos_w = t0 - WINDOW + jnp.arange(span)
        dist_w = (tq[:, None] - kpos_w[None]).astype(f32)
        mask_w = (dist_w >= 0) & (dist_w < WINDOW) & (kpos_w[None] >= 0)
        s_w = jnp.einsum('bqygd,bkyd->bygqk', qb, kw).astype(f32) * scale - slopes[:, :, None, None] * dist_w
        p_w = masked_softmax(s_w, mask_w)
        o_win = jnp.einsum('bygqk,bkyd->bqygd', p_w, vw)
        return o_sel, o_win

    o_sel, o_win = lax.map(one_block, jnp.arange(s_len // Q_BLOCK))
    o_sel = jnp.moveaxis(o_sel, 0, 1).reshape(bsz, s_len, hkv, grp, dh)
    o_win = jnp.moveaxis(o_win, 0, 1).reshape(bsz, s_len, hkv, grp, dh)
    g = jax.nn.sigmoid(gate_logits.astype(f32)).reshape(bsz, s_len, 3, hkv, grp)[..., None]
    o = g[:, :, 0] * o_cmp + g[:, :, 1] * o_sel + g[:, :, 2] * o_win
    return o.reshape(bsz, s_len, NSA_WIDTH)


def mla_mixer(c_q, c_kv, k_rope, positions, q_norm, kv_norm, w_uq, w_ukv):
    f32 = jnp.float32
    bsz, s_len = c_q.shape[:2]
    q = (rms_norm(c_q, q_norm) @ w_uq).reshape(bsz, s_len, MLA_HEADS, MLA_NOPE + MLA_ROPE)
    kv = (rms_norm(c_kv, kv_norm) @ w_ukv).reshape(bsz, s_len, MLA_HEADS, MLA_NOPE + MLA_V)
    q = jnp.concatenate([q[..., :MLA_NOPE], rope(q[..., MLA_NOPE:], positions)], axis=-1)
    k_r = jnp.broadcast_to(rope(k_rope[:, :, None, :], positions), (bsz, s_len, MLA_HEADS, MLA_ROPE))
    k = jnp.concatenate([kv[..., :MLA_NOPE], k_r], axis=-1)
    v = kv[..., MLA_NOPE:]
    scale = (MLA_NOPE + MLA_ROPE) ** -0.5
    kpos = jnp.arange(s_len)

    def one_block(i):
        t0 = i * Q_BLOCK
        qb = lax.dynamic_slice_in_dim(q, t0, Q_BLOCK, axis=1)
        s = jnp.einsum('bqhd,bkhd->bhqk', qb, k).astype(f32) * scale
        mask = kpos[None] <= (t0 + jnp.arange(Q_BLOCK))[:, None]
        return jnp.einsum('bhqk,bkhd->bqhd', masked_softmax(s, mask), v)

    o = lax.map(one_block, jnp.arange(s_len // Q_BLOCK))
    return jnp.moveaxis(o, 0, 1).reshape(bsz, s_len, MLA_WIDTH)


def odd_mixer(h, positions, w_in, cmp_pos_k, cmp_pos_v, ck_w1, ck_w2, cv_w1, cv_w2,
              q_norm, kv_norm, w_uq, w_ukv, w_out):
    bsz, s_len = h.shape[:2]
    q_n, kv_n, gl_n, g_n, c_q, c_kv, k_r, g_m = split_cols(h @ w_in, ODD_SPLITS)
    q_n = q_n.reshape(bsz, s_len, NSA_HEADS, NSA_HEAD_DIM)
    kv_n = kv_n.reshape(bsz, s_len, 6, NSA_KV_HEADS, NSA_HEAD_DIM)
    y_c = nsa_mixer(q_n, kv_n, gl_n, cmp_pos_k, cmp_pos_v, ck_w1, ck_w2, cv_w1, cv_w2) * jax.nn.silu(g_n)
    y_d = mla_mixer(c_q, c_kv, k_r, positions, q_norm, kv_norm, w_uq, w_ukv) * jax.nn.silu(g_m)
    return jnp.concatenate([y_c, y_d], axis=-1) @ w_out


def setup_inputs(seed: int = 0) -> dict:
    key = jax.random.key(seed)
    ks = iter(jax.random.split(key, 40))

    def nrm(shape, scale):
        return jax.random.normal(next(ks), shape, jnp.float32) * scale

    def gain(shape):
        return 1.0 + nrm(shape, 0.02)

    ne, no = N_EVEN, N_ODD
    g, n, c = S5_GROUPS, S5_STATE, S5_GROUP
    return {
        'x': nrm((BATCH, SEQ, D_MODEL), 1.0),
        'p': nrm((DEPTH, BATCH, SEQ, PLE_DIM), 1.0),
        'positions': jnp.broadcast_to(jnp.arange(SEQ, dtype=jnp.int32), (BATCH, SEQ)),
        'pre_norm': gain((DEPTH, D_MODEL)),
        'post_norm': gain((DEPTH, D_MODEL)),
        'ple_gate': nrm((DEPTH, D_MODEL, D_MODEL), D_MODEL ** -0.5),
        'ple_proj': nrm((DEPTH, PLE_DIM, D_MODEL), PLE_DIM ** -0.5),
        'ev_w_in': nrm((ne, D_MODEL, EVEN_IN), D_MODEL ** -0.5),
        's5_lam_re': -0.5 * jnp.exp(nrm((ne, g, n), 0.01)),
        's5_lam_im': jnp.pi * jnp.arange(n, dtype=jnp.float32) + nrm((ne, g, n), 0.01),
        's5_b_re': nrm((ne, g, n, c), (2 * c) ** -0.5),
        's5_b_im': nrm((ne, g, n, c), (2 * c) ** -0.5),
        's5_c_re': nrm((ne, g, c, n), (2 * n) ** -0.5),
        's5_c_im': nrm((ne, g, c, n), (2 * n) ** -0.5),
        's5_d': nrm((ne, S5_WIDTH), 0.5),
        's5_log_step': jax.random.uniform(next(ks), (ne, g), jnp.float32, math.log(1e-3), math.log(1e-1)),
        's5_w_glu': nrm((ne, S5_WIDTH, S5_WIDTH), S5_WIDTH ** -0.5),
        's5_b_glu': nrm((ne, S5_WIDTH), 0.01),
        'cv_w_dw': nrm((ne, CONV_K, CONV_WIDTH), CONV_K ** -0.5),
        'cv_b_dw': nrm((ne, CONV_WIDTH), 0.01),
        'cv_ln_g': gain((ne, CONV_WIDTH)),
        'cv_ln_b': nrm((ne, CONV_WIDTH), 0.01),
        'cv_w_pw': nrm((ne, CONV_WIDTH, CONV_WIDTH), CONV_WIDTH ** -0.5),
        'ev_w_out': nrm((ne, EVEN_MIX, D_MODEL), EVEN_MIX ** -0.5),
        'od_w_in': nrm((no, D_MODEL, ODD_IN), D_MODEL ** -0.5),
        'nsa_pos_k': nrm((no, CMP_LEN, NSA_HEAD_DIM), 0.1),
        'nsa_pos_v': nrm((no, CMP_LEN, NSA_HEAD_DIM), 0.1),
        'nsa_ck_w1': nrm((no, CMP_LEN * NSA_HEAD_DIM, CMP_HIDDEN), (CMP_LEN * NSA_HEAD_DIM) ** -0.5),
        'nsa_ck_w2': nrm((no, CMP_HIDDEN, NSA_HEAD_DIM), CMP_HIDDEN ** -0.5),
        'nsa_cv_w1': nrm((no, CMP_LEN * NSA_HEAD_DIM, CMP_HIDDEN), (CMP_LEN * NSA_HEAD_DIM) ** -0.5),
        'nsa_cv_w2': nrm((no, CMP_HIDDEN, NSA_HEAD_DIM), CMP_HIDDEN ** -0.5),
        'mla_q_norm': gain((no, MLA_Q_RANK)),
        'mla_kv_norm': gain((no, MLA_KV_RANK)),
        'mla_w_uq': nrm((no, MLA_Q_RANK, MLA_HEADS * (MLA_NOPE + MLA_ROPE)), MLA_Q_RANK ** -0.5),
        'mla_w_ukv': nrm((no, MLA_KV_RANK, MLA_HEADS * (MLA_NOPE + MLA_V)), MLA_KV_RANK ** -0.5),
        'od_w_out': nrm((no, ODD_MIX, D_MODEL), ODD_MIX ** -0.5),
    }


def reference(x, p, positions, pre_norm, post_norm, ple_gate, ple_proj,
              ev_w_in, s5_lam_re, s5_lam_im, s5_b_re, s5_b_im, s5_c_re, s5_c_im, s5_d, s5_log_step,
              s5_w_glu, s5_b_glu, cv_w_dw, cv_b_dw, cv_ln_g, cv_ln_b, cv_w_pw, ev_w_out,
              od_w_in, nsa_pos_k, nsa_pos_v, nsa_ck_w1, nsa_ck_w2, nsa_cv_w1, nsa_cv_w2,
              mla_q_norm, mla_kv_norm, mla_w_uq, mla_w_ukv, od_w_out):
    h = x
    for i in range(DEPTH):
        hn = rms_norm(h, pre_norm[i])
        j = i // 2
        if i % 2 == 0:
            y = even_mixer(hn, ev_w_in[j], s5_lam_re[j], s5_lam_im[j], s5_b_re[j], s5_b_im[j],
                           s5_c_re[j], s5_c_im[j], s5_d[j], s5_log_step[j], s5_w_glu[j], s5_b_glu[j],
                           cv_w_dw[j], cv_b_dw[j], cv_ln_g[j], cv_ln_b[j], cv_w_pw[j], ev_w_out[j])
        else:
            y = odd_mixer(hn, positions, od_w_in[j], nsa_pos_k[j], nsa_pos_v[j], nsa_ck_w1[j], nsa_ck_w2[j],
                          nsa_cv_w1[j], nsa_cv_w2[j], mla_q_norm[j], mla_kv_norm[j], mla_w_uq[j],
                          mla_w_ukv[j], od_w_out[j])
        h = h + rms_norm(y, post_norm[i])
        h = h + jax.nn.sigmoid(h @ ple_gate[i]) * (p[i] @ ple_proj[i])
    return h
```

```python
import functools
import math

import jax
import jax.numpy as jnp
import numpy as np
from jax import lax
from jax.experimental import pallas as pl
from jax.experimental.pallas import tpu as pltpu

F32 = jnp.float32
BF16 = jnp.bfloat16

D_MODEL = 1024
DEPTH = 4
PLE_DIM = 256
EPS = 1e-6
NEG = -1e30

S5_WIDTH = 512
S5_GROUP = 16
S5_GROUPS = 32
S5_STATE = 64
S5_HALF_GROUPS = 16
S5_HALF_STATE = S5_HALF_GROUPS * S5_STATE
CONV_WIDTH = 512
CONV_K = 31

NSA_HEADS = 8
NSA_KV_HEADS = 2
NSA_GQA = 4
NSA_HEAD_DIM = 64
NSA_WIDTH = 512
CMP_LEN = 32
CMP_STRIDE = 16
CMP_HIDDEN = 256
SEL_LEN = 64
SEL_TOPN = 8
WINDOW = 512

MLA_HEADS = 8
MLA_NOPE = 64
MLA_ROPE = 32
MLA_V = 64
MLA_Q_RANK = 256
MLA_KV_RANK = 128
MLA_WIDTH = 512
ROPE_BASE = 10000.0

LANE = 128
SUBLANE = 8
VMEM_LIMIT = 56 * 1024 * 1024

TM = 256
S5_TB = 64
CONV_TB = 64
CONV_ROWS = 32
NSA_TQ = 128
NSA_TK = 512
MLA_TQ = 256
MLA_TK = 512


def _cparams(sem):
    return pltpu.CompilerParams(dimension_semantics=sem, vmem_limit_bytes=VMEM_LIMIT)


def _dot(a, b):
    return jnp.dot(a, b, preferred_element_type=F32)


def _dot_nt(a, b):
    return lax.dot_general(a, b, (((1,), (1,)), ((), ())), preferred_element_type=F32)


def _rms(x, g):
    return x * lax.rsqrt(jnp.mean(x * x, axis=-1, keepdims=True) + EPS) * g


def _sigmoid(x):
    return 1.0 / (1.0 + jnp.exp(-x))


def _silu(x):
    return x * _sigmoid(x)


def _gelu_tanh(x):
    c = math.sqrt(2.0 / math.pi)
    return 0.5 * x * (1.0 + jnp.tanh(c * (x + 0.044715 * (x * x * x))))


def _masked_softmax(s, mask):
    s = jnp.where(mask, s, NEG)
    m = jnp.max(s, axis=-1, keepdims=True)
    e = jnp.where(mask, jnp.exp(s - m), 0.0)
    return e / jnp.maximum(jnp.sum(e, axis=-1, keepdims=True), 1e-30)


def _in_proj_kernel(segs, x_ref, g_ref, w_ref, *out_refs):
    xn = _rms(x_ref[...], g_ref[...]).astype(BF16)
    for (c0, c1), o_ref in zip(segs, out_refs):
        o_ref[...] = _dot(xn, w_ref[:, c0:c1]).astype(o_ref.dtype)


def _in_proj(h, h_batch_major, gain, w, segs, out_dtypes, out_time_major, bsz, s_len):
    n_t = s_len // TM
    if h_batch_major:
        x_spec = pl.BlockSpec((None, TM, D_MODEL), lambda b, i: (b, i, 0))
    else:
        x_spec = pl.BlockSpec((TM, D_MODEL), lambda b, i: (i, b))
    out_shapes, out_specs = [], []
    for (c0, c1), dt in zip(segs, out_dtypes):
        n = c1 - c0
        if out_time_major:
            out_shapes.append(jax.ShapeDtypeStruct((s_len, bsz * n), dt))
            out_specs.append(pl.BlockSpec((TM, n), lambda b, i: (i, b)))
        else:
            out_shapes.append(jax.ShapeDtypeStruct((bsz, s_len, n), dt))
            out_specs.append(pl.BlockSpec((None, TM, n), lambda b, i: (b, i, 0)))
    return pl.pallas_call(
        functools.partial(_in_proj_kernel, tuple(segs)),
        grid=(bsz, n_t),
        in_specs=[x_spec,
                  pl.BlockSpec((1, D_MODEL), lambda b, i: (0, 0)),
                  pl.BlockSpec(w.shape, lambda b, i: (0, 0))],
        out_specs=out_specs,
        out_shape=out_shapes,
        compiler_params=_cparams(("parallel", "parallel")),
        name="in_proj",
    )(h, gain.reshape(1, D_MODEL), w)


def _out_proj_kernel(ya_ref, yb_ref, h_ref, p_ref, wo_ref, gpost_ref, wg_ref, wp_ref, o_ref):
    half = wo_ref.shape[0] // 2
    y = _dot(ya_ref[...], wo_ref[0:half, :]) + _dot(yb_ref[...], wo_ref[half:, :])
    h1 = h_ref[...] + _rms(y, gpost_ref[...])
    gate = _sigmoid(_dot(h1.astype(BF16), wg_ref[...]))
    o_ref[...] = h1 + gate * _dot(p_ref[...].astype(BF16), wp_ref[...])


def _out_proj(ya, yb, y_time_major, h, h_batch_major, p, layer, w_out, g_post, w_gate, w_ple,
              out_batch_major, bsz, s_len):
    n_t = s_len // TM
    half = w_out.shape[0] // 2
    if y_time_major:
        y_spec = pl.BlockSpec((TM, half), lambda b, i: (i, b))
    else:
        y_spec = pl.BlockSpec((None, TM, half), lambda b, i: (b, i, 0))
    if h_batch_major:
        h_spec = pl.BlockSpec((None, TM, D_MODEL), lambda b, i: (b, i, 0))
    else:
        h_spec = pl.BlockSpec((TM, D_MODEL), lambda b, i: (i, b))
    if out_batch_major:
        o_shape = jax.ShapeDtypeStruct((bsz, s_len, D_MODEL), F32)
        o_spec = pl.BlockSpec((None, TM, D_MODEL), lambda b, i: (b, i, 0))
    else:
        o_shape = jax.ShapeDtypeStruct((s_len, bsz * D_MODEL), F32)
        o_spec = pl.BlockSpec((TM, D_MODEL), lambda b, i: (i, b))
    const = lambda b, i: (0, 0)
    return pl.pallas_call(
        _out_proj_kernel,
        grid=(bsz, n_t),
        in_specs=[y_spec, y_spec, h_spec,
                  pl.BlockSpec((None, None, TM, PLE_DIM), lambda b, i: (layer, b, i, 0)),
                  pl.BlockSpec(w_out.shape, const),
                  pl.BlockSpec((1, D_MODEL), const),
                  pl.BlockSpec(w_gate.shape, const),
                  pl.BlockSpec(w_ple.shape, const)],
        out_specs=o_spec,
        out_shape=o_shape,
        compiler_params=_cparams(("parallel", "parallel")),
        name="out_proj",
    )(ya, yb, h, p, w_out, g_post.reshape(1, D_MODEL), w_gate, w_ple)


def _s5_kernel(u_ref, ga_ref, wb_ref, are_ref, aim_ref, wc_ref, d_ref, wglu_ref, bglu_ref,
               o_ref, buf_ref, st_ref):
    rows = u_ref.shape[0]
    n_steps = rows // SUBLANE
    hs = S5_HALF_STATE

    @pl.when(pl.program_id(0) == 0)
    def _():
        st_ref[...] = jnp.zeros_like(st_ref)

    u = u_ref[...]
    ub = u.astype(BF16)
    ys = []
    for hf in range(2):
        c0 = hf * 2 * hs
        buf_ref[:, c0:c0 + 2 * hs] = _dot(ub[:, hf * 256:(hf + 1) * 256], wb_ref[hf])
        a_re = jnp.broadcast_to(are_ref[hf], (SUBLANE, hs))
        a_im = jnp.broadcast_to(aim_ref[hf], (SUBLANE, hs))

        def step(t, carry, c0=c0, a_re=a_re, a_im=a_im):
            x_re, x_im = carry
            r0 = pl.multiple_of(t * SUBLANE, SUBLANE)
            b_re = buf_ref[pl.ds(r0, SUBLANE), c0:c0 + hs]
            b_im = buf_ref[pl.ds(r0, SUBLANE), c0 + hs:c0 + 2 * hs]
            n_re = a_re * x_re - a_im * x_im + b_re
            n_im = a_re * x_im + a_im * x_re + b_im
            buf_ref[pl.ds(r0, SUBLANE), c0:c0 + hs] = n_re
            buf_ref[pl.ds(r0, SUBLANE), c0 + hs:c0 + 2 * hs] = n_im
            return n_re, n_im

        x_re, x_im = lax.fori_loop(
            0, n_steps, step, (st_ref[:, c0:c0 + hs], st_ref[:, c0 + hs:c0 + 2 * hs]))
        st_ref[:, c0:c0 + hs] = x_re
        st_ref[:, c0 + hs:c0 + 2 * hs] = x_im
        ys.append(_dot(buf_ref[:, c0:c0 + 2 * hs].astype(BF16), wc_ref[hf]))
    y = jnp.concatenate(ys, axis=-1) + d_ref[...] * u
    y = _gelu_tanh(y)
    y = y * _sigmoid(_dot(y.astype(BF16), wglu_ref[...]) + bglu_ref[...])
    o_ref[...] = (y * _silu(ga_ref[...])).astype(o_ref.dtype)


def _s5(u, ga, wb, a_re, a_im, wc, d_skip, w_glu, b_glu, bsz, s_len):
    rows = S5_TB * bsz
    n_blk = s_len // S5_TB
    const2 = lambda i: (0, 0)
    const3 = lambda i: (0, 0, 0)
    return pl.pallas_call(
        _s5_kernel,
        grid=(n_blk,),
        in_specs=[pl.BlockSpec((rows, S5_WIDTH), lambda i: (i, 0)),
                  pl.BlockSpec((rows, S5_WIDTH), lambda i: (i, 0)),
                  pl.BlockSpec(wb.shape, const3),
                  pl.BlockSpec(a_re.shape, const3),
                  pl.BlockSpec(a_im.shape, const3),
                  pl.BlockSpec(wc.shape, const3),
                  pl.BlockSpec((1, S5_WIDTH), const2),
                  pl.BlockSpec(w_glu.shape, const2),
                  pl.BlockSpec((1, S5_WIDTH), const2)],
        out_specs=pl.BlockSpec((rows, S5_WIDTH), lambda i: (i, 0)),
        out_shape=jax.ShapeDtypeStruct((s_len * bsz, S5_WIDTH), BF16),
        scratch_shapes=[pltpu.VMEM((rows, 4 * S5_HALF_STATE), F32),
                        pltpu.VMEM((SUBLANE, 4 * S5_HALF_STATE), F32)],
        compiler_params=_cparams(("arbitrary",)),
        name="s5",
    )(u, ga, wb, a_re, a_im, wc, d_skip.reshape(1, S5_WIDTH), w_glu, b_glu.reshape(1, S5_WIDTH))


def _s5_weights(lam_re, lam_im, b_re, b_im, c_re, c_im, log_step):
    step = jnp.exp(log_step.astype(F32))[:, None]
    lr, li = lam_re.astype(F32), lam_im.astype(F32)
    mag = jnp.exp(lr * step)
    ab_re, ab_im = mag * jnp.cos(li * step), mag * jnp.sin(li * step)
    den = lr * lr + li * li
    nr, ni = ab_re - 1.0, ab_im
    f_re, f_im = (nr * lr + ni * li) / den, (ni * lr - nr * li) / den
    br, bi = b_re.astype(F32), b_im.astype(F32)
    bb_re = f_re[..., None] * br - f_im[..., None] * bi
    bb_im = f_re[..., None] * bi + f_im[..., None] * br
    hg = S5_HALF_GROUPS
    eye = jnp.eye(hg, dtype=F32)

    def b_blockdiag(bb):
        bb = bb.reshape(2, hg, S5_STATE, S5_GROUP)
        return jnp.einsum('hgnc,gk->hgckn', bb, eye).reshape(2, hg * S5_GROUP, hg * S5_STATE)

    def c_blockdiag(cc):
        cc = cc.reshape(2, hg, S5_GROUP, S5_STATE)
        return jnp.einsum('hgcn,gk->hgnkc', cc, eye).reshape(2, hg * S5_STATE, hg * S5_GROUP)

    wb = jnp.concatenate([b_blockdiag(bb_re), b_blockdiag(bb_im)], axis=-1).astype(BF16)
    wc = jnp.concatenate([c_blockdiag(c_re.astype(F32)), -c_blockdiag(c_im.astype(F32))],
                         axis=1).astype(BF16)
    a_re = ab_re.reshape(2, 1, S5_HALF_STATE)
    a_im = ab_im.reshape(2, 1, S5_HALF_STATE)
    return wb, a_re, a_im, wc


def _conv_kernel(v_ref, gb_ref, wdw_ref, bdw_ref, lng_ref, lnb_ref, wpw_ref, o_ref, hbuf_ref, cbuf_ref):
    rows = v_ref.shape[0]
    halo = (CONV_K - 1) * SUBLANE

    @pl.when(pl.program_id(0) == 0)
    def _():
        hbuf_ref[0:halo, :] = jnp.zeros((halo, CONV_WIDTH), F32)

    @pl.when(pl.program_id(0) > 0)
    def _():
        hbuf_ref[0:halo, :] = hbuf_ref[rows:rows + halo, :]

    hbuf_ref[halo:halo + rows, :] = v_ref[:, 0:CONV_WIDTH] * _sigmoid(v_ref[:, CONV_WIDTH:])

    def chunk(c, carry):
        r0 = pl.multiple_of(c * CONV_ROWS, CONV_ROWS)
        acc = jnp.zeros((CONV_ROWS, CONV_WIDTH), F32)
        for k in range(CONV_K):
            acc = acc + wdw_ref[k:k + 1, :] * hbuf_ref[pl.ds(r0 + k * SUBLANE, CONV_ROWS), :]
        cbuf_ref[pl.ds(r0, CONV_ROWS), :] = acc
        return carry

    lax.fori_loop(0, rows // CONV_ROWS, chunk, 0)
    c = cbuf_ref[...] + bdw_ref[...]
    mu = jnp.mean(c, axis=-1, keepdims=True)
    var = jnp.mean(jnp.square(c - mu), axis=-1, keepdims=True)
    hn = _silu((c - mu) * lax.rsqrt(var + EPS) * lng_ref[...] + lnb_ref[...])
    y = _dot(hn.astype(BF16), wpw_ref[...])
    o_ref[...] = (y * _silu(gb_ref[...])).astype(o_ref.dtype)


def _conv(v, gb, w_dw, b_dw, ln_g, ln_b, w_pw, bsz, s_len):
    rows = CONV_TB * bsz
    n_blk = s_len // CONV_TB
    halo = (CONV_K - 1) * SUBLANE
    const2 = lambda i: (0, 0)
    vec = lambda a: a.reshape(1, CONV_WIDTH)
    return pl.pallas_call(
        _conv_kernel,
        grid=(n_blk,),
        in_specs=[pl.BlockSpec((rows, 2 * CONV_WIDTH), lambda i: (i, 0)),
                  pl.BlockSpec((rows, CONV_WIDTH), lambda i: (i, 0)),
                  pl.BlockSpec((CONV_K, CONV_WIDTH), const2),
                  pl.BlockSpec((1, CONV_WIDTH), const2),
                  pl.BlockSpec((1, CONV_WIDTH), const2),
                  pl.BlockSpec((1, CONV_WIDTH), const2),
                  pl.BlockSpec(w_pw.shape, const2)],
        out_specs=pl.BlockSpec((rows, CONV_WIDTH), lambda i: (i, 0)),
        out_shape=jax.ShapeDtypeStruct((s_len * bsz, CONV_WIDTH), BF16),
        scratch_shapes=[pltpu.VMEM((halo + rows, CONV_WIDTH), F32),
                        pltpu.VMEM((rows, CONV_WIDTH), F32)],
        compiler_params=_cparams(("arbitrary",)),
        name="conv",
    )(v, gb, w_dw, vec(b_dw), vec(ln_g), vec(ln_b), w_pw)


def _nsa_compress_kernel(ck_ref, cv_ref, pk_ref, pv_ref, kw1_ref, kw2_ref, vw1_ref, vw2_ref, ko_ref, vo_ref):
    half = CMP_STRIDE * NSA_HEAD_DIM
    n_chunks = ck_ref.shape[0]

    def compress(c_ref, pos_ref, w1_ref, w2_ref):
        c = c_ref[...]
        lo = _dot((c + pos_ref[:, 0:half]).astype(BF16), w1_ref[0:half, :])
        hi = _dot((c + pos_ref[:, half:]).astype(BF16), w1_ref[half:, :])
        hid = lo + pltpu.roll(hi, n_chunks - 1, 0)
        return _dot(_gelu_tanh(hid).astype(BF16), w2_ref[...])

    ko_ref[...] = compress(ck_ref, pk_ref, kw1_ref, kw2_ref).astype(ko_ref.dtype)
    vo_ref[...] = compress(cv_ref, pv_ref, vw1_ref, vw2_ref).astype(vo_ref.dtype)


def _nsa_compress(chunks, pos_k, pos_v, kw1, kw2, vw1, vw2, bsz, n_chunks):
    width = CMP_STRIDE * NSA_HEAD_DIM
    const2 = lambda b, y: (0, 0)
    o_shape = jax.ShapeDtypeStruct((bsz, NSA_KV_HEADS, n_chunks, LANE), BF16)
    o_spec = pl.BlockSpec((None, None, n_chunks, LANE), lambda b, y: (b, y, 0, 0))
    return pl.pallas_call(
        _nsa_compress_kernel,
        grid=(bsz, NSA_KV_HEADS),
        in_specs=[pl.BlockSpec((None, None, None, n_chunks, width), lambda b, y: (b, 0, y, 0, 0)),
                  pl.BlockSpec((None, None, None, n_chunks, width), lambda b, y: (b, 1, y, 0, 0)),
                  pl.BlockSpec(pos_k.shape, const2),
                  pl.BlockSpec(pos_v.shape, const2),
                  pl.BlockSpec(kw1.shape, const2),
                  pl.BlockSpec(kw2.shape, const2),
                  pl.BlockSpec(vw1.shape, const2),
                  pl.BlockSpec(vw2.shape, const2)],
        out_specs=[o_spec, o_spec],
        out_shape=[o_shape, o_shape],
        compiler_params=_cparams(("parallel", "parallel")),
        name="nsa_compress",
    )(chunks, chunks, pos_k, pos_v, kw1, kw2, vw1, vw2)


def _nsa_attn_kernel(s_len, q_ref, kc_ref, vc_ref, ks_ref, vs_ref, kw_ref, vw_ref, gl_ref, gn_ref,
                     ov_ref, ex_ref, o_ref, m_ref, l_ref, acc_ref):
    tq = NSA_TQ
    y = pl.program_id(1)
    i = pl.program_id(2)
    t0 = i * tq
    scale = NSA_HEAD_DIM ** -0.5
    rows = NSA_GQA * tq
    n_cmp = (s_len - CMP_LEN) // CMP_STRIDE + 1

    qs = jnp.concatenate([q_ref[:, g * LANE:(g + 1) * LANE] for g in range(NSA_GQA)], axis=0)
    slopes = [jnp.where(y == 0, 2.0 ** -(g + 1), 2.0 ** -(NSA_GQA + g + 1)).astype(F32)
              for g in range(NSA_GQA)]

    def alibi(s, dist):
        return jnp.concatenate(
            [s[g * tq:(g + 1) * tq] * scale - slopes[g] * dist for g in range(NSA_GQA)], axis=0)

    def stack(a):
        return jnp.concatenate([a] * NSA_GQA, axis=0)

    def pair_merge(o):
        lane = lax.broadcasted_iota(jnp.int32, (tq, LANE), 1)
        lo = lane < NSA_HEAD_DIM
        return jnp.concatenate([jnp.where(lo, o[0:tq], o[tq:2 * tq]),
                                jnp.where(lo, o[2 * tq:3 * tq], o[3 * tq:4 * tq])], axis=-1)

    tpos = t0 + lax.broadcasted_iota(jnp.int32, (tq, LANE), 0)
    lane_i = lax.broadcasted_iota(jnp.int32, (tq, LANE), 1)

    dist_c = tpos - (lane_i * CMP_STRIDE + (CMP_LEN - 1))
    mask_c = stack((dist_c >= 0) & (lane_i < n_cmp))
    s_c = alibi(_dot_nt(qs, kc_ref[...]), dist_c.astype(F32))
    p_c = _masked_softmax(s_c, mask_c)
    o_cmp = _dot(p_c.astype(BF16), vc_ref[...])

    p_sum = p_c[0:tq] + p_c[tq:2 * tq] + p_c[2 * tq:3 * tq] + p_c[3 * tq:4 * tq]
    p_hi = p_sum.astype(BF16)
    p_lo = (p_sum - p_hi.astype(F32)).astype(BF16)
    p_slc = _dot(p_hi, ov_ref[...]) + _dot(p_lo, ov_ref[...])
    cur = tpos // SEL_LEN
    forced = (lane_i == 0) | (lane_i == cur) | (lane_i == cur - 1)
    causal_blk = lane_i * SEL_LEN <= tpos
    score = jnp.where(forced, p_slc + 1e4, jnp.where(causal_blk, p_slc, -1e4))
    n_sel_blocks = s_len // SEL_LEN
    score = jnp.where(lane_i < n_sel_blocks, score, -3e38)
    lane_f = lane_i.astype(F32)
    sel = jnp.zeros((tq, LANE), F32)
    for _ in range(min(SEL_TOPN, n_sel_blocks)):
        mx = jnp.max(score, axis=-1, keepdims=True)
        first = jnp.min(jnp.where(score == mx, lane_f, float(LANE)), axis=-1, keepdims=True)
        pick = lane_f == first
        sel = jnp.where(pick, 1.0, sel)
        score = jnp.where(pick, -3e38, score)
    sel_b = sel.astype(BF16)

    tk = NSA_TK
    m_ref[...] = jnp.full_like(m_ref, NEG)
    l_ref[...] = jnp.zeros_like(l_ref)
    acc_ref[...] = jnp.zeros_like(acc_ref)
    tq_k = t0 + lax.broadcasted_iota(jnp.int32, (tq, tk), 0)
    col_k = lax.broadcasted_iota(jnp.int32, (tq, tk), 1)

    def sel_tile(kt, carry):
        k0 = pl.multiple_of(kt * tk, tk)
        k = ks_ref[pl.ds(k0, tk), :]
        v = vs_ref[pl.ds(k0, tk), :]
        dist = tq_k - (k0 + col_k)
        chosen = _dot_nt(sel_b, ex_ref[pl.ds(k0, tk), :])
        valid = stack((chosen > 0.5) & (dist >= 0))
        s = jnp.where(valid, alibi(_dot_nt(qs, k), dist.astype(F32)), NEG)
        m_old = m_ref[...]
        m_new = jnp.maximum(m_old, jnp.max(s, axis=-1, keepdims=True))
        p = jnp.where(valid, jnp.exp(s - m_new), 0.0)
        alpha = jnp.exp(m_old - m_new)
        l_ref[...] = alpha * l_ref[...] + jnp.sum(p, axis=-1, keepdims=True)
        acc_ref[...] = alpha * acc_ref[...] + _dot(p.astype(BF16), v)
        m_ref[...] = m_new
        return carry

    lax.fori_loop(0, (t0 + tq + tk - 1) // tk, sel_tile, 0)
    o_sel = acc_ref[...] / jnp.maximum(l_ref[...], 1e-30)

    span = min(WINDOW + tq, s_len)
    ws = pl.multiple_of(jnp.minimum(jnp.maximum(t0 - WINDOW, 0), s_len - span), tq)
    kw = kw_ref[pl.ds(ws, span), :]
    vw = vw_ref[pl.ds(ws, span), :]
    dist_w = (t0 + lax.broadcasted_iota(jnp.int32, (tq, span), 0)
              - (ws + lax.broadcasted_iota(jnp.int32, (tq, span), 1)))
    mask_w = stack((dist_w >= 0) & (dist_w < WINDOW))
    p_w = _masked_softmax(alibi(_dot_nt(qs, kw), dist_w.astype(F32)), mask_w)
    o_win = _dot(p_w.astype(BF16), vw)

    gates = _sigmoid(gl_ref[...])
    outs = []
    for g in range(NSA_GQA):
        r = slice(g * tq, (g + 1) * tq)
        outs.append(gates[:, g:g + 1] * o_cmp[r]
                    + gates[:, NSA_GQA + g:NSA_GQA + g + 1] * o_sel[r]
                    + gates[:, 2 * NSA_GQA + g:2 * NSA_GQA + g + 1] * o_win[r])
    o = pair_merge(jnp.concatenate(outs, axis=0))
    o_ref[...] = (o * _silu(gn_ref[...])).astype(o_ref.dtype)


def _nsa_attn(q, kc, vc, kvn, gl, gn, overlap, expand, bsz, s_len):
    tq = NSA_TQ
    n_chunks = kc.shape[2]
    rows = NSA_GQA * tq
    kv_spec = lambda col: pl.BlockSpec((None, s_len, LANE), lambda b, y, i, col=col: (b, 0, col + y))
    c_spec = pl.BlockSpec((None, None, n_chunks, LANE), lambda b, y, i: (b, y, 0, 0))
    return pl.pallas_call(
        functools.partial(_nsa_attn_kernel, s_len),
        grid=(bsz, NSA_KV_HEADS, s_len // tq),
        in_specs=[pl.BlockSpec((None, tq, NSA_GQA * LANE), lambda b, y, i: (b, i, y)),
                  c_spec, c_spec,
                  kv_spec(0), kv_spec(2), kv_spec(4), kv_spec(6),
                  pl.BlockSpec((None, tq, LANE), lambda b, y, i: (b, i, y)),
                  pl.BlockSpec((None, tq, NSA_GQA * NSA_HEAD_DIM), lambda b, y, i: (b, i, y)),
                  pl.BlockSpec(overlap.shape, lambda b, y, i: (0, 0)),
                  pl.BlockSpec(expand.shape, lambda b, y, i: (0, 0))],
        out_specs=pl.BlockSpec((None, tq, NSA_GQA * NSA_HEAD_DIM), lambda b, y, i: (b, i, y)),
        out_shape=jax.ShapeDtypeStruct((bsz, s_len, NSA_WIDTH), BF16),
        scratch_shapes=[pltpu.VMEM((rows, 1), F32), pltpu.VMEM((rows, 1), F32),
                        pltpu.VMEM((rows, LANE), F32)],
        compiler_params=_cparams(("parallel", "parallel", "parallel")),
        name="nsa_attn",
    )(q, kc, vc, kvn, kvn, kvn, kvn, gl, gn, overlap, expand)


def _rope_table_kernel(pos_ref, freq_ref, cos_ref, sin_ref):
    ang = pos_ref[...].astype(F32) * freq_ref[...]
    lane = lax.broadcasted_iota(jnp.int32, ang.shape, 1)
    half = MLA_ROPE // 2
    first = (lane >= MLA_NOPE) & (lane < MLA_NOPE + half)
    second = (lane >= MLA_NOPE + half) & (lane < MLA_NOPE + MLA_ROPE)
    cos_ref[...] = jnp.where(first | second, jnp.cos(ang), jnp.where(lane < MLA_NOPE, 1.0, 0.0))
    sn = jnp.sin(ang)
    sin_ref[...] = jnp.where(first, -sn, jnp.where(second, sn, 0.0))


def _rope_tables(positions, bsz, s_len):
    half = MLA_ROPE // 2
    freqs = ROPE_BASE ** (-jnp.arange(half, dtype=F32) / half)
    freq_lane = jnp.concatenate([jnp.zeros((MLA_NOPE,), F32), freqs, freqs,
                                 jnp.zeros((LANE - MLA_NOPE - MLA_ROPE,), F32)]).reshape(1, LANE)
    spec = pl.BlockSpec((None, TM, LANE), lambda b, i: (b, i, 0))
    shape = jax.ShapeDtypeStruct((bsz, s_len, LANE), F32)
    return pl.pallas_call(
        _rope_table_kernel,
        grid=(bsz, s_len // TM),
        in_specs=[pl.BlockSpec((None, TM, 1), lambda b, i: (b, i, 0)),
                  pl.BlockSpec((1, LANE), lambda b, i: (0, 0))],
        out_specs=[spec, spec],
        out_shape=[shape, shape],
        compiler_params=_cparams(("parallel", "parallel")),
        name="rope_tables",
    )(positions.reshape(bsz, s_len, 1), freq_lane)


def _mla_proj_kernel(cq_ref, ckr_ref, cos_ref, sin_ref, qn_ref, kvn_ref, wqa_ref, wqb_ref, wk_ref, wv_ref,
                     q_ref, k_ref, v_ref):
    cq = _rms(cq_ref[...], qn_ref[...]).astype(BF16)
    ckv = _rms(ckr_ref[:, 0:MLA_KV_RANK], kvn_ref[...]).astype(BF16)
    cosm = cos_ref[...]
    sinm = sin_ref[...]
    kr = ckr_ref[:, MLA_KV_RANK:MLA_KV_RANK + LANE] * cosm + ckr_ref[:, MLA_KV_RANK + LANE:] * sinm
    v_ref[...] = _dot(ckv, wv_ref[...]).astype(v_ref.dtype)
    for h in range(MLA_HEADS):
        c = slice(h * LANE, (h + 1) * LANE)
        q_ref[:, c] = (_dot(cq, wqa_ref[:, c]) * cosm + _dot(cq, wqb_ref[:, c]) * sinm).astype(q_ref.dtype)
        k_ref[:, c] = (_dot(ckv, wk_ref[:, c]) + kr).astype(k_ref.dtype)


def _mla_proj(cq, ckr, cosm, sinm, q_norm, kv_norm, wqa, wqb, wk, wv, bsz, s_len):
    const = lambda b, i: (0, 0)
    tok = lambda n: pl.BlockSpec((None, TM, n), lambda b, i: (b, i, 0))
    return pl.pallas_call(
        _mla_proj_kernel,
        grid=(bsz, s_len // TM),
        in_specs=[tok(MLA_Q_RANK), tok(MLA_KV_RANK + 2 * LANE), tok(LANE), tok(LANE),
                  pl.BlockSpec((1, MLA_Q_RANK), const), pl.BlockSpec((1, MLA_KV_RANK), const),
                  pl.BlockSpec(wqa.shape, const), pl.BlockSpec(wqb.shape, const),
                  pl.BlockSpec(wk.shape, const), pl.BlockSpec(wv.shape, const)],
        out_specs=[tok(MLA_HEADS * LANE), tok(MLA_HEADS * LANE), tok(MLA_WIDTH)],
        out_shape=[jax.ShapeDtypeStruct((bsz, s_len, MLA_HEADS * LANE), BF16),
                   jax.ShapeDtypeStruct((bsz, s_len, MLA_HEADS * LANE), BF16),
                   jax.ShapeDtypeStruct((bsz, s_len, MLA_WIDTH), BF16)],
        compiler_params=_cparams(("parallel", "parallel")),
        name="mla_proj",
    )(cq, ckr, cosm, sinm, q_norm.reshape(1, MLA_Q_RANK), kv_norm.reshape(1, MLA_KV_RANK), wqa, wqb, wk, wv)


def _mla_attn_kernel(q_ref, k_ref, v_ref, gm_ref, o_ref, m_ref, l_ref, acc_ref):
    tq, tk = MLA_TQ, MLA_TK
    t0 = pl.program_id(2) * tq
    scale = (MLA_NOPE + MLA_ROPE) ** -0.5
    tq_k = t0 + lax.broadcasted_iota(jnp.int32, (tq, tk), 0)
    col_k = lax.broadcasted_iota(jnp.int32, (tq, tk), 1)
    heads = []
    for hh in range(2):
        q = q_ref[:, hh * LANE:(hh + 1) * LANE]
        m_ref[...] = jnp.full_like(m_ref, NEG)
        l_ref[...] = jnp.zeros_like(l_ref)
        acc_ref[...] = jnp.zeros_like(acc_ref)

        def tile(kt, carry, q=q, hh=hh):
            k0 = pl.multiple_of(kt * tk, tk)
            k = k_ref[pl.ds(k0, tk), hh * LANE:(hh + 1) * LANE]
            v = v_ref[pl.ds(k0, tk), :]
            valid = (k0 + col_k) <= tq_k
            s = jnp.where(valid, _dot_nt(q, k) * scale, NEG)
            m_old = m_ref[...]
            m_new = jnp.maximum(m_old, jnp.max(s, axis=-1, keepdims=True))
            p = jnp.where(valid, jnp.exp(s - m_new), 0.0)
            alpha = jnp.exp(m_old - m_new)
            l_ref[...] = alpha * l_ref[...] + jnp.sum(p, axis=-1, keepdims=True)
            acc_ref[...] = alpha * acc_ref[...] + _dot(p.astype(BF16), v)
            m_ref[...] = m_new
            return carry

        lax.fori_loop(0, (t0 + tq + tk - 1) // tk, tile, 0)
        heads.append(acc_ref[...] / jnp.maximum(l_ref[...], 1e-30))
    lane = lax.broadcasted_iota(jnp.int32, (tq, LANE), 1)
    o = jnp.where(lane < MLA_V, heads[0], heads[1])
    o_ref[...] = (o * _silu(gm_ref[...])).astype(o_ref.dtype)


def _mla_attn(q, k, v, gm, bsz, s_len):
    tq = MLA_TQ
    return pl.pallas_call(
        _mla_attn_kernel,
        grid=(bsz, MLA_HEADS // 2, s_len // tq),
        in_specs=[pl.BlockSpec((None, tq, 2 * LANE), lambda b, hp, i: (b, i, hp)),
                  pl.BlockSpec((None, s_len, 2 * LANE), lambda b, hp, i: (b, 0, hp)),
                  pl.BlockSpec((None, s_len, LANE), lambda b, hp, i: (b, 0, hp)),
                  pl.BlockSpec((None, tq, LANE), lambda b, hp, i: (b, i, hp))],
        out_specs=pl.BlockSpec((None, tq, LANE), lambda b, hp, i: (b, i, hp)),
        out_shape=jax.ShapeDtypeStruct((bsz, s_len, MLA_WIDTH), BF16),
        scratch_shapes=[pltpu.VMEM((tq, 1), F32), pltpu.VMEM((tq, 1), F32), pltpu.VMEM((tq, LANE), F32)],
        compiler_params=_cparams(("parallel", "parallel", "parallel")),
        name="mla_attn",
    )(q, k, v, gm)


def _odd_in_layout():
    off = np.cumsum([0, NSA_WIDTH, 6 * NSA_KV_HEADS * NSA_HEAD_DIM, 3 * NSA_HEADS, NSA_WIDTH,
                     MLA_Q_RANK, MLA_KV_RANK, MLA_ROPE, MLA_WIDTH])
    q0, kv0, gl0, gn0, cq0, ckv0, kr0, gm0 = off[:8]
    dh = NSA_HEAD_DIM
    zeros = lambda n: [-1] * n
    cols, segs, start = [], [], 0

    def seg(c):
        nonlocal start
        cols.extend(c)
        segs.append((start, start + len(c)))
        start += len(c)

    c = []
    for h in range(NSA_HEADS):
        c += list(range(q0 + h * dh, q0 + (h + 1) * dh)) + zeros(LANE - dh)
    seg(c)
    seg(list(range(kv0, kv0 + 2 * NSA_KV_HEADS * dh)))
    c = []
    for slot in range(2, 6):
        for y in range(NSA_KV_HEADS):
            base = kv0 + (slot * NSA_KV_HEADS + y) * dh
            src = list(range(base, base + dh))
            c += src + (src if slot % 2 == 1 else zeros(LANE - dh))
    seg(c)
    c = []
    for y in range(NSA_KV_HEADS):
        blk = [gl0 + br * NSA_HEADS + y * NSA_GQA + g for br in range(3) for g in range(NSA_GQA)]
        c += blk + zeros(LANE - len(blk))
    seg(c)
    seg(list(range(gn0, gn0 + NSA_WIDTH)))
    seg(list(range(cq0, cq0 + MLA_Q_RANK)))
    half = MLA_ROPE // 2
    kr = list(range(kr0, kr0 + MLA_ROPE))
    pad_hi = zeros(LANE - MLA_NOPE - MLA_ROPE)
    seg(list(range(ckv0, ckv0 + MLA_KV_RANK))
        + zeros(MLA_NOPE) + kr + pad_hi
        + zeros(MLA_NOPE) + kr[half:] + kr[:half] + pad_hi)
    seg(list(range(gm0, gm0 + MLA_WIDTH)))
    return np.asarray(cols, np.int32), segs


_ODD_COLS, _ODD_SEGS = _odd_in_layout()
_ODD_DTYPES = (BF16, F32, BF16, F32, F32, F32, F32, F32)


def _gather_cols(w, cols):
    return jnp.where(jnp.asarray(cols >= 0)[None, :], w[:, np.maximum(cols, 0)], 0.0)


def _mla_weight_layout():
    dq = MLA_NOPE + MLA_ROPE
    half = MLA_ROPE // 2
    qa, qb, kn, vv = [], [], [], []
    for h in range(MLA_HEADS):
        nope = list(range(h * dq, h * dq + MLA_NOPE))
        rope = list(range(h * dq + MLA_NOPE, (h + 1) * dq))
        pad = [-1] * (LANE - dq)
        qa += nope + rope + pad
        qb += [-1] * MLA_NOPE + rope[half:] + rope[:half] + pad
        kv0 = h * (MLA_NOPE + MLA_V)
        kn += list(range(kv0, kv0 + MLA_NOPE)) + [-1] * (LANE - MLA_NOPE)
        vv += list(range(kv0 + MLA_NOPE, kv0 + MLA_NOPE + MLA_V))
    return tuple(np.asarray(c, np.int32) for c in (qa, qb, kn, vv))


_MLA_QA, _MLA_QB, _MLA_KN, _MLA_VV = _mla_weight_layout()


def _nsa_constants(s_len):
    n_cmp = (s_len - CMP_LEN) // CMP_STRIDE + 1
    n_sel = s_len // SEL_LEN
    cs = np.arange(LANE)[:, None] * CMP_STRIDE
    sb = np.arange(LANE)[None]
    overlap = ((cs < (sb + 1) * SEL_LEN) & (cs + CMP_LEN > sb * SEL_LEN)
               & (np.arange(LANE)[:, None] < n_cmp) & (sb < n_sel)).astype(np.float32)
    expand = ((np.arange(s_len)[:, None] // SEL_LEN) == np.arange(LANE)[None]).astype(np.float32)
    return jnp.asarray(overlap, BF16), jnp.asarray(expand, BF16)


_EVEN_SEGS = ((0, 512), (512, 1024), (1024, 2048), (2048, 2560))


def _even_layer(h, h_batch_major, p, layer, out_batch_major, bsz, s_len, pre_g, post_g, w_gate, w_ple,
                w_in, lam_re, lam_im, b_re, b_im, c_re, c_im, d_skip, log_step, w_glu, b_glu,
                w_dw, b_dw, ln_g, ln_b, w_pw, w_out):
    ua, ga, vb, gb = _in_proj(h, h_batch_major, pre_g, w_in.astype(BF16), _EVEN_SEGS, (F32,) * 4, True,
                              bsz, s_len)
    rows = s_len * bsz
    wb, a_re, a_im, wc = _s5_weights(lam_re, lam_im, b_re, b_im, c_re, c_im, log_step)
    ya = _s5(ua.reshape(rows, S5_WIDTH), ga.reshape(rows, S5_WIDTH), wb, a_re, a_im, wc, d_skip,
             w_glu.astype(BF16), b_glu, bsz, s_len)
    yb = _conv(vb.reshape(rows, 2 * CONV_WIDTH), gb.reshape(rows, CONV_WIDTH), w_dw, b_dw, ln_g, ln_b,
               w_pw.astype(BF16), bsz, s_len)
    return _out_proj(ya.reshape(s_len, bsz * S5_WIDTH), yb.reshape(s_len, bsz * CONV_WIDTH), True,
                     h, h_batch_major, p, layer, w_out.astype(BF16), post_g, w_gate.astype(BF16),
                     w_ple.astype(BF16), out_batch_major, bsz, s_len)


def _odd_layer(h, p, layer, out_batch_major, bsz, s_len, rope_cos, rope_sin, pre_g, post_g, w_gate, w_ple,
               w_in, pos_k, pos_v, ck_w1, ck_w2, cv_w1, cv_w2, q_norm, kv_norm, w_uq, w_ukv, w_out):
    w_in_p = _gather_cols(w_in, _ODD_COLS).astype(BF16)
    q, kcv, kvn, gl, gn, cq, ckr, gm = _in_proj(h, False, pre_g, w_in_p, _ODD_SEGS, _ODD_DTYPES, False,
                                                bsz, s_len)
    n_chunks = s_len // CMP_STRIDE
    chunks = kcv.reshape(bsz, n_chunks, CMP_STRIDE, 2, NSA_KV_HEADS, NSA_HEAD_DIM)
    chunks = jnp.transpose(chunks, (0, 3, 4, 1, 2, 5)).reshape(
        bsz, 2, NSA_KV_HEADS, n_chunks, CMP_STRIDE * NSA_HEAD_DIM)
    pad = jnp.zeros((CMP_HIDDEN, LANE - NSA_HEAD_DIM), F32)
    kw2 = jnp.concatenate([ck_w2, pad], axis=1).astype(BF16)
    vw2 = jnp.concatenate([cv_w2, cv_w2], axis=1).astype(BF16)
    kc, vc = _nsa_compress(chunks, pos_k.reshape(1, -1), pos_v.reshape(1, -1), ck_w1.astype(BF16), kw2,
                           cv_w1.astype(BF16), vw2, bsz, n_chunks)
    overlap, expand = _nsa_constants(s_len)
    y_c = _nsa_attn(q, kc, vc, kvn, gl, gn, overlap, expand, bsz, s_len)
    qm, km, vm = _mla_proj(cq, ckr, rope_cos, rope_sin, q_norm, kv_norm,
                           _gather_cols(w_uq, _MLA_QA).astype(BF16), _gather_cols(w_uq, _MLA_QB).astype(BF16),
                           _gather_cols(w_ukv, _MLA_KN).astype(BF16), w_ukv[:, _MLA_VV].astype(BF16),
                           bsz, s_len)
    y_d = _mla_attn(qm, km, vm, gm, bsz, s_len)
    return _out_proj(y_c, y_d, False, h, False, p, layer, w_out.astype(BF16), post_g, w_gate.astype(BF16),
                     w_ple.astype(BF16), out_batch_major, bsz, s_len)


def kernel(x, p, positions, pre_norm, post_norm, ple_gate, ple_proj, ev_w_in, s5_lam_re, s5_lam_im, s5_b_re, s5_b_im, s5_c_re, s5_c_im, s5_d, s5_log_step, s5_w_glu, s5_b_glu, cv_w_dw, cv_b_dw, cv_ln_g, cv_ln_b, cv_w_pw, ev_w_out, od_w_in, nsa_pos_k, nsa_pos_v, nsa_ck_w1, nsa_ck_w2, nsa_cv_w1, nsa_cv_w2, mla_q_norm, mla_kv_norm, mla_w_uq, mla_w_ukv, od_w_out):
    bsz, s_len, _ = x.shape
    assert bsz == SUBLANE and s_len % max(TM, NSA_TK, MLA_TK) == 0
    depth = p.shape[0]
    rope_cos, rope_sin = _rope_tables(positions, bsz, s_len)
    h = x
    for i in range(depth):
        j = i // 2
        last = i == depth - 1
        if i % 2 == 0:
            h = _even_layer(h, i == 0, p, i, last, bsz, s_len, pre_norm[i], post_norm[i], ple_gate[i],
                            ple_proj[i], ev_w_in[j], s5_lam_re[j], s5_lam_im[j], s5_b_re[j], s5_b_im[j],
                            s5_c_re[j], s5_c_im[j], s5_d[j], s5_log_step[j], s5_w_glu[j], s5_b_glu[j],
                            cv_w_dw[j], cv_b_dw[j], cv_ln_g[j], cv_ln_b[j], cv_w_pw[j], ev_w_out[j])
        else:
            h = _odd_layer(h, p, i, last, bsz, s_len, rope_cos, rope_sin, pre_norm[i], post_norm[i],
                           ple_gate[i], ple_proj[i], od_w_in[j], nsa_pos_k[j], nsa_pos_v[j], nsa_ck_w1[j],
                           nsa_ck_w2[j], nsa_cv_w1[j], nsa_cv_w2[j], mla_q_norm[j], mla_kv_norm[j],
                           mla_w_uq[j], mla_w_ukv[j], od_w_out[j])
    return h
```

```python
import functools
import math

import jax
import jax.numpy as jnp
import numpy as np
from jax import lax
from jax.experimental import pallas as pl
from jax.experimental.pallas import tpu as pltpu

F32 = jnp.float32
BF16 = jnp.bfloat16

D_MODEL = 1024
DEPTH = 4
PLE_DIM = 256
EPS = 1e-6
NEG = -1e30

S5_WIDTH = 512
S5_GROUP = 16
S5_GROUPS = 32
S5_STATE = 64
S5_HALF_GROUPS = 16
S5_HALF_STATE = S5_HALF_GROUPS * S5_STATE
CONV_WIDTH = 512
CONV_K = 31

NSA_HEADS = 8
NSA_KV_HEADS = 2
NSA_GQA = 4
NSA_HEAD_DIM = 64
NSA_WIDTH = 512
CMP_LEN = 32
CMP_STRIDE = 16
CMP_HIDDEN = 256
SEL_LEN = 64
SEL_TOPN = 8
WINDOW = 512

MLA_HEADS = 8
MLA_NOPE = 64
MLA_ROPE = 32
MLA_V = 64
MLA_Q_RANK = 256
MLA_KV_RANK = 128
MLA_WIDTH = 512
ROPE_BASE = 10000.0

LANE = 128
SUBLANE = 8
VMEM_LIMIT = 56 * 1024 * 1024

TM = 256
S5_TB = 64
CONV_TB = 64
CONV_ROWS = 32
NSA_TQ = 128
NSA_TK = 512
MLA_TQ = 256
MLA_TK = 512
MLA_SUB_ROWS = 128

NSA_SEL_LANES = 32
NSA_D0 = NSA_SEL_LANES
NSA_POS_HI = NSA_D0 + NSA_HEAD_DIM
NSA_POS_LO = NSA_POS_HI + 1
MASKED = -1e30
M_INIT = -1e29
MLA_SCALE = (MLA_NOPE + MLA_ROPE) ** -0.5


def _cparams(sem):
    return pltpu.CompilerParams(dimension_semantics=sem, vmem_limit_bytes=VMEM_LIMIT)


def _dot(a, b):
    return jnp.dot(a, b, preferred_element_type=F32)


def _dot_nt(a, b):
    return lax.dot_general(a, b, (((1,), (1,)), ((), ())), preferred_element_type=F32)


def _rms(x, g):
    return x * lax.rsqrt(jnp.mean(x * x, axis=-1, keepdims=True) + EPS) * g


def _sigmoid(x):
    return 1.0 / (1.0 + jnp.exp(-x))


def _silu(x):
    return x * _sigmoid(x)


def _gelu_tanh(x):
    c = math.sqrt(2.0 / math.pi)
    return 0.5 * x * (1.0 + jnp.tanh(c * (x + 0.044715 * (x * x * x))))


def _masked_softmax(s, mask):
    s = jnp.where(mask, s, NEG)
    m = jnp.max(s, axis=-1, keepdims=True)
    e = jnp.where(mask, jnp.exp(s - m), 0.0)
    return e / jnp.maximum(jnp.sum(e, axis=-1, keepdims=True), 1e-30)


def _flash_tile(scores, v, m_ref, l_ref, acc_ref, row_slices):
    n_rep = scores[0].shape[1] // LANE
    stats = []
    for s, r in zip(scores, row_slices):
        m_old = m_ref[r, :]
        m_new = jnp.maximum(m_old, jnp.max(s, axis=-1, keepdims=True))
        p = jnp.exp(s - jnp.concatenate([m_new] * n_rep, axis=1))
        stats.append((p.astype(BF16), jnp.exp(m_old - m_new), jnp.sum(p, axis=-1, keepdims=True), m_new))
    pvs = [_dot(p, v) for p, _, _, _ in stats]
    for (_, alpha, row_sum, m_new), pv, r in zip(stats, pvs, row_slices):
        l_ref[r, :] = alpha * l_ref[r, :] + row_sum
        acc_ref[r, :] = alpha * acc_ref[r, :] + pv
        m_ref[r, :] = m_new


def _in_proj_kernel(segs, tab_seg, x_ref, g_ref, w_ref, *refs):
    tab_ref, out_refs = (None, refs) if tab_seg is None else (refs[0], refs[1:])
    xn = _rms(x_ref[...], g_ref[...]).astype(BF16)
    for n, ((c0, c1), o_ref) in enumerate(zip(segs, out_refs)):
        y = _dot(xn, w_ref[:, c0:c1])
        if n == tab_seg:
            y = y + tab_ref[...]
        o_ref[...] = y.astype(o_ref.dtype)


def _in_proj(h, h_batch_major, gain, w, segs, out_dtypes, out_time_major, bsz, s_len, tab_seg=None, tab=None):
    n_t = s_len // TM
    if h_batch_major:
        x_spec = pl.BlockSpec((None, TM, D_MODEL), lambda b, i: (b, i, 0))
    else:
        x_spec = pl.BlockSpec((TM, D_MODEL), lambda b, i: (i, b))
    out_shapes, out_specs = [], []
    for (c0, c1), dt in zip(segs, out_dtypes):
        n = c1 - c0
        if out_time_major:
            out_shapes.append(jax.ShapeDtypeStruct((s_len, bsz * n), dt))
            out_specs.append(pl.BlockSpec((TM, n), lambda b, i: (i, b)))
        else:
            out_shapes.append(jax.ShapeDtypeStruct((bsz, s_len, n), dt))
            out_specs.append(pl.BlockSpec((None, TM, n), lambda b, i: (b, i, 0)))
    in_specs = [x_spec,
                pl.BlockSpec((1, D_MODEL), lambda b, i: (0, 0)),
                pl.BlockSpec(w.shape, lambda b, i: (0, 0))]
    args = [h, gain.reshape(1, D_MODEL), w]
    if tab_seg is not None:
        in_specs.append(pl.BlockSpec((TM, tab.shape[1]), lambda b, i: (i, 0)))
        args.append(tab)
    return pl.pallas_call(
        functools.partial(_in_proj_kernel, tuple(segs), tab_seg),
        grid=(bsz, n_t),
        in_specs=in_specs,
        out_specs=out_specs,
        out_shape=out_shapes,
        compiler_params=_cparams(("parallel", "parallel")),
        name="in_proj",
    )(*args)


def _out_proj_kernel(ya_ref, yb_ref, h_ref, p_ref, wo_ref, gpost_ref, wg_ref, wp_ref, o_ref):
    half = wo_ref.shape[0] // 2
    y = _dot(ya_ref[...], wo_ref[0:half, :]) + _dot(yb_ref[...], wo_ref[half:, :])
    h1 = h_ref[...] + _rms(y, gpost_ref[...])
    gate = _sigmoid(_dot(h1.astype(BF16), wg_ref[...]))
    o_ref[...] = h1 + gate * _dot(p_ref[...].astype(BF16), wp_ref[...])


def _out_proj(ya, yb, y_time_major, h, h_batch_major, p, layer, w_out, g_post, w_gate, w_ple,
              out_batch_major, bsz, s_len):
    n_t = s_len // TM
    half = w_out.shape[0] // 2
    if y_time_major:
        y_spec = pl.BlockSpec((TM, half), lambda b, i: (i, b))
    else:
        y_spec = pl.BlockSpec((None, TM, half), lambda b, i: (b, i, 0))
    if h_batch_major:
        h_spec = pl.BlockSpec((None, TM, D_MODEL), lambda b, i: (b, i, 0))
    else:
        h_spec = pl.BlockSpec((TM, D_MODEL), lambda b, i: (i, b))
    if out_batch_major:
        o_shape = jax.ShapeDtypeStruct((bsz, s_len, D_MODEL), F32)
        o_spec = pl.BlockSpec((None, TM, D_MODEL), lambda b, i: (b, i, 0))
    else:
        o_shape = jax.ShapeDtypeStruct((s_len, bsz * D_MODEL), F32)
        o_spec = pl.BlockSpec((TM, D_MODEL), lambda b, i: (i, b))
    const = lambda b, i: (0, 0)
    return pl.pallas_call(
        _out_proj_kernel,
        grid=(bsz, n_t),
        in_specs=[y_spec, y_spec, h_spec,
                  pl.BlockSpec((None, None, TM, PLE_DIM), lambda b, i: (layer, b, i, 0)),
                  pl.BlockSpec(w_out.shape, const),
                  pl.BlockSpec((1, D_MODEL), const),
                  pl.BlockSpec(w_gate.shape, const),
                  pl.BlockSpec(w_ple.shape, const)],
        out_specs=o_spec,
        out_shape=o_shape,
        compiler_params=_cparams(("parallel", "parallel")),
        name="out_proj",
    )(ya, yb, h, p, w_out, g_post.reshape(1, D_MODEL), w_gate, w_ple)


def _s5_kernel(u_ref, ga_ref, wb_ref, are_ref, aim_ref, wc_ref, d_ref, wglu_ref, bglu_ref,
               o_ref, buf_ref, st_ref):
    rows = u_ref.shape[0]
    n_steps = rows // SUBLANE
    hs = S5_HALF_STATE

    @pl.when(pl.program_id(0) == 0)
    def _():
        st_ref[...] = jnp.zeros_like(st_ref)

    u = u_ref[...]
    ub = u.astype(BF16)
    ys = []
    for hf in range(2):
        c0 = hf * 2 * hs
        buf_ref[:, c0:c0 + 2 * hs] = _dot(ub[:, hf * 256:(hf + 1) * 256], wb_ref[hf])
        a_re = jnp.broadcast_to(are_ref[hf], (SUBLANE, hs))
        a_im = jnp.broadcast_to(aim_ref[hf], (SUBLANE, hs))

        def step(t, carry, c0=c0, a_re=a_re, a_im=a_im):
            x_re, x_im = carry
            r0 = pl.multiple_of(t * SUBLANE, SUBLANE)
            b_re = buf_ref[pl.ds(r0, SUBLANE), c0:c0 + hs]
            b_im = buf_ref[pl.ds(r0, SUBLANE), c0 + hs:c0 + 2 * hs]
            n_re = a_re * x_re - a_im * x_im + b_re
            n_im = a_re * x_im + a_im * x_re + b_im
            buf_ref[pl.ds(r0, SUBLANE), c0:c0 + hs] = n_re
            buf_ref[pl.ds(r0, SUBLANE), c0 + hs:c0 + 2 * hs] = n_im
            return n_re, n_im

        x_re, x_im = lax.fori_loop(
            0, n_steps, step, (st_ref[:, c0:c0 + hs], st_ref[:, c0 + hs:c0 + 2 * hs]))
        st_ref[:, c0:c0 + hs] = x_re
        st_ref[:, c0 + hs:c0 + 2 * hs] = x_im
        ys.append(_dot(buf_ref[:, c0:c0 + 2 * hs].astype(BF16), wc_ref[hf]))
    y = jnp.concatenate(ys, axis=-1) + d_ref[...] * u
    y = _gelu_tanh(y)
    y = y * _sigmoid(_dot(y.astype(BF16), wglu_ref[...]) + bglu_ref[...])
    o_ref[...] = (y * _silu(ga_ref[...])).astype(o_ref.dtype)


def _s5(u, ga, wb, a_re, a_im, wc, d_skip, w_glu, b_glu, bsz, s_len):
    rows = S5_TB * bsz
    n_blk = s_len // S5_TB
    const2 = lambda i: (0, 0)
    const3 = lambda i: (0, 0, 0)
    return pl.pallas_call(
        _s5_kernel,
        grid=(n_blk,),
        in_specs=[pl.BlockSpec((rows, S5_WIDTH), lambda i: (i, 0)),
                  pl.BlockSpec((rows, S5_WIDTH), lambda i: (i, 0)),
                  pl.BlockSpec(wb.shape, const3),
                  pl.BlockSpec(a_re.shape, const3),
                  pl.BlockSpec(a_im.shape, const3),
                  pl.BlockSpec(wc.shape, const3),
                  pl.BlockSpec((1, S5_WIDTH), const2),
                  pl.BlockSpec(w_glu.shape, const2),
                  pl.BlockSpec((1, S5_WIDTH), const2)],
        out_specs=pl.BlockSpec((rows, S5_WIDTH), lambda i: (i, 0)),
        out_shape=jax.ShapeDtypeStruct((s_len * bsz, S5_WIDTH), BF16),
        scratch_shapes=[pltpu.VMEM((rows, 4 * S5_HALF_STATE), F32),
                        pltpu.VMEM((SUBLANE, 4 * S5_HALF_STATE), F32)],
        compiler_params=_cparams(("arbitrary",)),
        name="s5",
    )(u, ga, wb, a_re, a_im, wc, d_skip.reshape(1, S5_WIDTH), w_glu, b_glu.reshape(1, S5_WIDTH))


def _s5_weights(lam_re, lam_im, b_re, b_im, c_re, c_im, log_step):
    step = jnp.exp(log_step.astype(F32))[:, None]
    lr, li = lam_re.astype(F32), lam_im.astype(F32)
    mag = jnp.exp(lr * step)
    ab_re, ab_im = mag * jnp.cos(li * step), mag * jnp.sin(li * step)
    den = lr * lr + li * li
    nr, ni = ab_re - 1.0, ab_im
    f_re, f_im = (nr * lr + ni * li) / den, (ni * lr - nr * li) / den
    br, bi = b_re.astype(F32), b_im.astype(F32)
    bb_re = f_re[..., None] * br - f_im[..., None] * bi
    bb_im = f_re[..., None] * bi + f_im[..., None] * br
    hg = S5_HALF_GROUPS
    eye = jnp.eye(hg, dtype=F32)

    def b_blockdiag(bb):
        bb = bb.reshape(2, hg, S5_STATE, S5_GROUP)
        return jnp.einsum('hgnc,gk->hgckn', bb, eye).reshape(2, hg * S5_GROUP, hg * S5_STATE)

    def c_blockdiag(cc):
        cc = cc.reshape(2, hg, S5_GROUP, S5_STATE)
        return jnp.einsum('hgcn,gk->hgnkc', cc, eye).reshape(2, hg * S5_STATE, hg * S5_GROUP)

    wb = jnp.concatenate([b_blockdiag(bb_re), b_blockdiag(bb_im)], axis=-1).astype(BF16)
    wc = jnp.concatenate([c_blockdiag(c_re.astype(F32)), -c_blockdiag(c_im.astype(F32))],
                         axis=1).astype(BF16)
    a_re = ab_re.reshape(2, 1, S5_HALF_STATE)
    a_im = ab_im.reshape(2, 1, S5_HALF_STATE)
    return wb, a_re, a_im, wc


def _conv_kernel(v_ref, gb_ref, wdw_ref, bdw_ref, lng_ref, lnb_ref, wpw_ref, o_ref, hbuf_ref, cbuf_ref):
    rows = v_ref.shape[0]
    halo = (CONV_K - 1) * SUBLANE

    @pl.when(pl.program_id(0) == 0)
    def _():
        hbuf_ref[0:halo, :] = jnp.zeros((halo, CONV_WIDTH), F32)

    @pl.when(pl.program_id(0) > 0)
    def _():
        hbuf_ref[0:halo, :] = hbuf_ref[rows:rows + halo, :]

    hbuf_ref[halo:halo + rows, :] = v_ref[:, 0:CONV_WIDTH] * _sigmoid(v_ref[:, CONV_WIDTH:])

    def chunk(c, carry):
        r0 = pl.multiple_of(c * CONV_ROWS, CONV_ROWS)
        acc = jnp.zeros((CONV_ROWS, CONV_WIDTH), F32)
        for k in range(CONV_K):
            acc = acc + wdw_ref[k:k + 1, :] * hbuf_ref[pl.ds(r0 + k * SUBLANE, CONV_ROWS), :]
        cbuf_ref[pl.ds(r0, CONV_ROWS), :] = acc
        return carry

    lax.fori_loop(0, rows // CONV_ROWS, chunk, 0)
    c = cbuf_ref[...] + bdw_ref[...]
    mu = jnp.mean(c, axis=-1, keepdims=True)
    var = jnp.mean(jnp.square(c - mu), axis=-1, keepdims=True)
    hn = _silu((c - mu) * lax.rsqrt(var + EPS) * lng_ref[...] + lnb_ref[...])
    y = _dot(hn.astype(BF16), wpw_ref[...])
    o_ref[...] = (y * _silu(gb_ref[...])).astype(o_ref.dtype)


def _conv(v, gb, w_dw, b_dw, ln_g, ln_b, w_pw, bsz, s_len):
    rows = CONV_TB * bsz
    n_blk = s_len // CONV_TB
    halo = (CONV_K - 1) * SUBLANE
    const2 = lambda i: (0, 0)
    vec = lambda a: a.reshape(1, CONV_WIDTH)
    return pl.pallas_call(
        _conv_kernel,
        grid=(n_blk,),
        in_specs=[pl.BlockSpec((rows, 2 * CONV_WIDTH), lambda i: (i, 0)),
                  pl.BlockSpec((rows, CONV_WIDTH), lambda i: (i, 0)),
                  pl.BlockSpec((CONV_K, CONV_WIDTH), const2),
                  pl.BlockSpec((1, CONV_WIDTH), const2),
                  pl.BlockSpec((1, CONV_WIDTH), const2),
                  pl.BlockSpec((1, CONV_WIDTH), const2),
                  pl.BlockSpec(w_pw.shape, const2)],
        out_specs=pl.BlockSpec((rows, CONV_WIDTH), lambda i: (i, 0)),
        out_shape=jax.ShapeDtypeStruct((s_len * bsz, CONV_WIDTH), BF16),
        scratch_shapes=[pltpu.VMEM((halo + rows, CONV_WIDTH), F32),
                        pltpu.VMEM((rows, CONV_WIDTH), F32)],
        compiler_params=_cparams(("arbitrary",)),
        name="conv",
    )(v, gb, w_dw, vec(b_dw), vec(ln_g), vec(ln_b), w_pw)


def _nsa_compress_kernel(ck_ref, cv_ref, pk_ref, pv_ref, kw1_ref, kw2_ref, vw1_ref, vw2_ref, ktab_ref,
                         ko_ref, vo_ref):
    half = CMP_STRIDE * NSA_HEAD_DIM
    n_chunks = ck_ref.shape[0]

    def compress(c_ref, pos_ref, w1_ref, w2_ref):
        c = c_ref[...]
        lo = _dot((c + pos_ref[:, 0:half]).astype(BF16), w1_ref[0:half, :])
        hi = _dot((c + pos_ref[:, half:]).astype(BF16), w1_ref[half:, :])
        hid = lo + pltpu.roll(hi, n_chunks - 1, 0)
        return _dot(_gelu_tanh(hid).astype(BF16), w2_ref[...])

    ko_ref[...] = (compress(ck_ref, pk_ref, kw1_ref, kw2_ref) + ktab_ref[...]).astype(ko_ref.dtype)
    vo_ref[...] = compress(cv_ref, pv_ref, vw1_ref, vw2_ref).astype(vo_ref.dtype)


def _nsa_compress(chunks, pos_k, pos_v, kw1, kw2, vw1, vw2, ktab, bsz, n_chunks):
    width = CMP_STRIDE * NSA_HEAD_DIM
    const2 = lambda b, y: (0, 0)
    o_shape = jax.ShapeDtypeStruct((bsz, NSA_KV_HEADS, n_chunks, LANE), BF16)
    o_spec = pl.BlockSpec((None, None, n_chunks, LANE), lambda b, y: (b, y, 0, 0))
    return pl.pallas_call(
        _nsa_compress_kernel,
        grid=(bsz, NSA_KV_HEADS),
        in_specs=[pl.BlockSpec((None, None, None, n_chunks, width), lambda b, y: (b, 0, y, 0, 0)),
                  pl.BlockSpec((None, None, None, n_chunks, width), lambda b, y: (b, 1, y, 0, 0)),
                  pl.BlockSpec(pos_k.shape, const2),
                  pl.BlockSpec(pos_v.shape, const2),
                  pl.BlockSpec(kw1.shape, const2),
                  pl.BlockSpec(kw2.shape, const2),
                  pl.BlockSpec(vw1.shape, const2),
                  pl.BlockSpec(vw2.shape, const2),
                  pl.BlockSpec(ktab.shape, const2)],
        out_specs=[o_spec, o_spec],
        out_shape=[o_shape, o_shape],
        compiler_params=_cparams(("parallel", "parallel")),
        name="nsa_compress",
    )(chunks, chunks, pos_k, pos_v, kw1, kw2, vw1, vw2, ktab)


def _nsa_attn_kernel(s_len, q_ref, kc_ref, vc_ref, ks_ref, vs_ref, kw_ref, vw_ref, gl_ref, gn_ref,
                     ov_ref, o_ref, m_ref, l_ref, acc_ref):
    tq, tk = NSA_TQ, NSA_TK
    y = pl.program_id(1)
    t0 = pl.program_id(2) * tq
    n_cmp = (s_len - CMP_LEN) // CMP_STRIDE + 1
    n_sel_blocks = s_len // SEL_LEN
    n_top = min(SEL_TOPN, n_sel_blocks)
    lane_i = lax.broadcasted_iota(jnp.int32, (tq, LANE), 1)
    tpos = t0 + lax.broadcasted_iota(jnp.int32, (tq, LANE), 0)
    slopes = [jnp.where(y == 0, 2.0 ** -(g + 1), 2.0 ** -(NSA_GQA + g + 1)).astype(F32)
              for g in range(NSA_GQA)]

    heads = range(NSA_GQA)

    def q_heads(extra):
        parts = []
        for g in heads:
            aug = jnp.where(lane_i == NSA_POS_HI, slopes[g] * 256.0,
                            jnp.where(lane_i == NSA_POS_LO, slopes[g], 0.0))
            if extra is not None:
                aug = aug + extra
            parts.append(q_ref[:, g * LANE:(g + 1) * LANE] + aug.astype(BF16))
        return parts

    mask_c = (tpos >= lane_i * CMP_STRIDE + (CMP_LEN - 1)) & (lane_i < n_cmp)
    kc, vc = kc_ref[...], vc_ref[...]
    p_c = [_masked_softmax(_dot_nt(q, kc), mask_c) for q in q_heads(None)]
    o_cmp = [_dot(p.astype(BF16), vc) for p in p_c]

    p_sum = (p_c[0] + p_c[1]) + (p_c[2] + p_c[3])
    p_hi = p_sum.astype(BF16)
    p_lo = (p_sum - p_hi.astype(F32)).astype(BF16)
    p_slc = _dot(p_hi, ov_ref[...]) + _dot(p_lo, ov_ref[...])
    cur = tpos // SEL_LEN
    forced = (lane_i == 0) | (lane_i == cur) | (lane_i == cur - 1)
    causal_blk = lane_i * SEL_LEN <= tpos
    score = jnp.where(forced, p_slc + 1e4, jnp.where(causal_blk, p_slc, -1e4))
    score_t = score.T[0:n_sel_blocks, :]
    j_idx = lax.broadcasted_iota(jnp.int32, score_t.shape, 0)
    rank = jnp.zeros(score_t.shape, F32)
    for blk in range(n_sel_blocks):
        row = score_t[blk:blk + 1, :]
        beats = (row > score_t) | ((row == score_t) & (j_idx > blk))
        rank = rank + jnp.where(beats, 1.0, 0.0)
    bias_t = jnp.where(rank < n_top, 0.0, MASKED)
    sel_bias = jnp.concatenate([bias_t, jnp.zeros((LANE - n_sel_blocks, tq), F32)], axis=0).T

    qs = q_heads(sel_bias)
    m_ref[...] = jnp.full_like(m_ref, M_INIT)
    l_ref[...] = jnp.zeros_like(l_ref)
    acc_ref[...] = jnp.zeros_like(acc_ref)

    def sel_tile(kt, causal_mask):
        k0 = pl.multiple_of(kt * tk, tk)
        k = ks_ref[pl.ds(k0, tk), :]
        v = vs_ref[pl.ds(k0, tk), :]
        scores = [_dot_nt(qs[g], k) for g in heads]
        if causal_mask:
            ok = (k0 + lax.broadcasted_iota(jnp.int32, (tq, tk), 1)
                  <= t0 + lax.broadcasted_iota(jnp.int32, (tq, tk), 0))
            scores = [jnp.where(ok, s, MASKED) for s in scores]
        _flash_tile(scores, v, m_ref, l_ref, acc_ref, [slice(g * tq, (g + 1) * tq) for g in heads])

    n_full = t0 // tk

    def full_tile(kt, carry):
        sel_tile(kt, False)
        return carry

    lax.fori_loop(0, n_full, full_tile, 0)
    sel_tile(n_full, True)

    span = min(WINDOW + tq, s_len)
    ws = pl.multiple_of(jnp.minimum(jnp.maximum(t0 - WINDOW, 0), s_len - span), tq)
    dist_w = ((t0 - ws) + lax.broadcasted_iota(jnp.int32, (tq, span), 0)
              - lax.broadcasted_iota(jnp.int32, (tq, span), 1))
    bias_w = jnp.where((dist_w >= 0) & (dist_w < WINDOW), 0.0, MASKED)
    kw = kw_ref[pl.ds(ws, span), :]
    vw = vw_ref[pl.ds(ws, span), :]
    s_w = [_dot_nt(qs[g], kw) + bias_w for g in heads]
    e_w = [jnp.exp(s - jnp.max(s, axis=-1, keepdims=True)) for s in s_w]
    sum_w = [jnp.maximum(jnp.sum(e, axis=-1, keepdims=True), 1e-30) for e in e_w]
    o_win = [_dot(e.astype(BF16), vw) / d for e, d in zip(e_w, sum_w)]

    gates = _sigmoid(gl_ref[...])
    outs = []
    for g in heads:
        r = slice(g * tq, (g + 1) * tq)
        o_sel = acc_ref[r, :] / jnp.maximum(l_ref[r, :], 1e-30)
        outs.append(gates[:, g:g + 1] * o_cmp[g]
                    + gates[:, NSA_GQA + g:NSA_GQA + g + 1] * o_sel
                    + gates[:, 2 * NSA_GQA + g:2 * NSA_GQA + g + 1] * o_win[g])
    lo = lane_i < NSA_HEAD_DIM
    o = jnp.concatenate([jnp.where(lo, outs[0], outs[1]), jnp.where(lo, outs[2], outs[3])], axis=-1)
    o_ref[...] = (o * _silu(gn_ref[...])).astype(o_ref.dtype)


def _nsa_attn(q, kc, vc, kvn, gl, gn, overlap, bsz, s_len):
    tq = NSA_TQ
    n_chunks = kc.shape[2]
    rows = NSA_GQA * tq
    kv_spec = lambda col: pl.BlockSpec((None, s_len, LANE), lambda b, y, i, col=col: (b, 0, col + y))
    c_spec = pl.BlockSpec((None, None, n_chunks, LANE), lambda b, y, i: (b, y, 0, 0))
    return pl.pallas_call(
        functools.partial(_nsa_attn_kernel, s_len),
        grid=(bsz, NSA_KV_HEADS, s_len // tq),
        in_specs=[pl.BlockSpec((None, tq, NSA_GQA * LANE), lambda b, y, i: (b, i, y)),
                  c_spec, c_spec,
                  kv_spec(0), kv_spec(2), kv_spec(4), kv_spec(6),
                  pl.BlockSpec((None, tq, LANE), lambda b, y, i: (b, i, y)),
                  pl.BlockSpec((None, tq, NSA_GQA * NSA_HEAD_DIM), lambda b, y, i: (b, i, y)),
                  pl.BlockSpec(overlap.shape, lambda b, y, i: (0, 0))],
        out_specs=pl.BlockSpec((None, tq, NSA_GQA * NSA_HEAD_DIM), lambda b, y, i: (b, i, y)),
        out_shape=jax.ShapeDtypeStruct((bsz, s_len, NSA_WIDTH), BF16),
        scratch_shapes=[pltpu.VMEM((rows, LANE), F32)] * 3,
        compiler_params=_cparams(("parallel", "parallel", "parallel")),
        name="nsa_attn",
    )(q, kc, vc, kvn, kvn, kvn, kvn, gl, gn, overlap)


def _rope_table_kernel(pos_ref, freq_ref, cos_ref, sin_ref):
    ang = pos_ref[...].astype(F32) * freq_ref[...]
    lane = lax.broadcasted_iota(jnp.int32, ang.shape, 1)
    half = MLA_ROPE // 2
    first = (lane >= MLA_NOPE) & (lane < MLA_NOPE + half)
    second = (lane >= MLA_NOPE + half) & (lane < MLA_NOPE + MLA_ROPE)
    cos_ref[...] = jnp.where(first | second, jnp.cos(ang), jnp.where(lane < MLA_NOPE, 1.0, 0.0))
    sn = jnp.sin(ang)
    sin_ref[...] = jnp.where(first, -sn, jnp.where(second, sn, 0.0))


def _rope_tables(positions, bsz, s_len):
    half = MLA_ROPE // 2
    freqs = ROPE_BASE ** (-jnp.arange(half, dtype=F32) / half)
    freq_lane = jnp.concatenate([jnp.zeros((MLA_NOPE,), F32), freqs, freqs,
                                 jnp.zeros((LANE - MLA_NOPE - MLA_ROPE,), F32)]).reshape(1, LANE)
    spec = pl.BlockSpec((None, TM, LANE), lambda b, i: (b, i, 0))
    shape = jax.ShapeDtypeStruct((bsz, s_len, LANE), F32)
    return pl.pallas_call(
        _rope_table_kernel,
        grid=(bsz, s_len // TM),
        in_specs=[pl.BlockSpec((None, TM, 1), lambda b, i: (b, i, 0)),
                  pl.BlockSpec((1, LANE), lambda b, i: (0, 0))],
        out_specs=[spec, spec],
        out_shape=[shape, shape],
        compiler_params=_cparams(("parallel", "parallel")),
        name="rope_tables",
    )(positions.reshape(bsz, s_len, 1), freq_lane)


def _mla_proj_kernel(cq_ref, ckr_ref, cos_ref, sin_ref, qn_ref, kvn_ref, wqa_ref, wqb_ref, wk_ref, wv_ref,
                     q_ref, k_ref, v_ref):
    cq = _rms(cq_ref[...], qn_ref[...]).astype(BF16)
    ckv = _rms(ckr_ref[:, 0:MLA_KV_RANK], kvn_ref[...]).astype(BF16)
    cosm = cos_ref[...]
    sinm = sin_ref[...]
    kr = ckr_ref[:, MLA_KV_RANK:MLA_KV_RANK + LANE] * cosm + ckr_ref[:, MLA_KV_RANK + LANE:] * sinm
    v_ref[...] = _dot(ckv, wv_ref[...]).astype(v_ref.dtype)
    for h in range(MLA_HEADS):
        c = slice(h * LANE, (h + 1) * LANE)
        q = _dot(cq, wqa_ref[:, c]) * cosm + _dot(cq, wqb_ref[:, c]) * sinm
        q_ref[:, c] = (q * MLA_SCALE).astype(q_ref.dtype)
        k_ref[:, c] = (_dot(ckv, wk_ref[:, c]) + kr).astype(k_ref.dtype)


def _mla_proj(cq, ckr, cosm, sinm, q_norm, kv_norm, wqa, wqb, wk, wv, bsz, s_len):
    const = lambda b, i: (0, 0)
    tok = lambda n: pl.BlockSpec((None, TM, n), lambda b, i: (b, i, 0))
    return pl.pallas_call(
        _mla_proj_kernel,
        grid=(bsz, s_len // TM),
        in_specs=[tok(MLA_Q_RANK), tok(MLA_KV_RANK + 2 * LANE), tok(LANE), tok(LANE),
                  pl.BlockSpec((1, MLA_Q_RANK), const), pl.BlockSpec((1, MLA_KV_RANK), const),
                  pl.BlockSpec(wqa.shape, const), pl.BlockSpec(wqb.shape, const),
                  pl.BlockSpec(wk.shape, const), pl.BlockSpec(wv.shape, const)],
        out_specs=[tok(MLA_HEADS * LANE), tok(MLA_HEADS * LANE), tok(MLA_WIDTH)],
        out_shape=[jax.ShapeDtypeStruct((bsz, s_len, MLA_HEADS * LANE), BF16),
                   jax.ShapeDtypeStruct((bsz, s_len, MLA_HEADS * LANE), BF16),
                   jax.ShapeDtypeStruct((bsz, s_len, MLA_WIDTH), BF16)],
        compiler_params=_cparams(("parallel", "parallel")),
        name="mla_proj",
    )(cq, ckr, cosm, sinm, q_norm.reshape(1, MLA_Q_RANK), kv_norm.reshape(1, MLA_KV_RANK), wqa, wqb, wk, wv)


def _mla_attn_kernel(q_ref, k_ref, v_ref, gm_ref, o_ref, m_ref, l_ref, acc_ref):
    tq, tk = MLA_TQ, MLA_TK
    t0 = pl.program_id(2) * tq
    n_full = t0 // tk
    sub = MLA_SUB_ROWS
    n_sub = tq // sub
    m_ref[...] = jnp.full_like(m_ref, M_INIT)
    l_ref[...] = jnp.zeros_like(l_ref)
    acc_ref[...] = jnp.zeros_like(acc_ref)

    def tile(kt, causal_mask):
        k0 = pl.multiple_of(kt * tk, tk)
        v = v_ref[pl.ds(k0, tk), :]
        scores, row_slices = [], []
        for hh in range(2):
            k = k_ref[pl.ds(k0, tk), hh * LANE:(hh + 1) * LANE]
            for rb in range(n_sub):
                s = _dot_nt(q_ref[rb * sub:(rb + 1) * sub, hh * LANE:(hh + 1) * LANE], k)
                if causal_mask:
                    ok = (k0 + lax.broadcasted_iota(jnp.int32, (sub, tk), 1)
                          <= t0 + rb * sub + lax.broadcasted_iota(jnp.int32, (sub, tk), 0))
                    s = jnp.where(ok, s, MASKED)
                scores.append(s)
                row_slices.append(slice(hh * tq + rb * sub, hh * tq + (rb + 1) * sub))
        _flash_tile(scores, v, m_ref, l_ref, acc_ref, row_slices)

    def full_tile(kt, carry):
        tile(kt, False)
        return carry

    lax.fori_loop(0, n_full, full_tile, 0)
    tile(n_full, True)
    o_all = acc_ref[...] / jnp.maximum(l_ref[...], 1e-30)
    lane = lax.broadcasted_iota(jnp.int32, (tq, LANE), 1)
    o = jnp.where(lane < MLA_V, o_all[0:tq], o_all[tq:2 * tq])
    o_ref[...] = (o * _silu(gm_ref[...])).astype(o_ref.dtype)


def _mla_attn(q, k, v, gm, bsz, s_len):
    tq = MLA_TQ
    return pl.pallas_call(
        _mla_attn_kernel,
        grid=(bsz, MLA_HEADS // 2, s_len // tq),
        in_specs=[pl.BlockSpec((None, tq, 2 * LANE), lambda b, hp, i: (b, i, hp)),
                  pl.BlockSpec((None, s_len, 2 * LANE), lambda b, hp, i: (b, 0, hp)),
                  pl.BlockSpec((None, s_len, LANE), lambda b, hp, i: (b, 0, hp)),
                  pl.BlockSpec((None, tq, LANE), lambda b, hp, i: (b, i, hp))],
        out_specs=pl.BlockSpec((None, tq, LANE), lambda b, hp, i: (b, i, hp)),
        out_shape=jax.ShapeDtypeStruct((bsz, s_len, MLA_WIDTH), BF16),
        scratch_shapes=[pltpu.VMEM((2 * tq, LANE), F32)] * 3,
        compiler_params=_cparams(("parallel", "parallel", "parallel")),
        name="mla_attn",
    )(q, k, v, gm)


def _odd_in_layout():
    off = np.cumsum([0, NSA_WIDTH, 6 * NSA_KV_HEADS * NSA_HEAD_DIM, 3 * NSA_HEADS, NSA_WIDTH,
                     MLA_Q_RANK, MLA_KV_RANK, MLA_ROPE, MLA_WIDTH])
    q0, kv0, gl0, gn0, cq0, ckv0, kr0, gm0 = off[:8]
    dh = NSA_HEAD_DIM
    zeros = lambda n: [-1] * n
    cols, segs, start = [], [], 0

    def seg(c):
        nonlocal start
        cols.extend(c)
        segs.append((start, start + len(c)))
        start += len(c)

    lead, tail = zeros(NSA_D0), zeros(LANE - NSA_D0 - dh)
    c = []
    for h in range(NSA_HEADS):
        c += lead + list(range(q0 + h * dh, q0 + (h + 1) * dh)) + tail
    seg(c)
    seg(list(range(kv0, kv0 + 2 * NSA_KV_HEADS * dh)))
    c = []
    for slot in range(2, 6):
        for y in range(NSA_KV_HEADS):
            base = kv0 + (slot * NSA_KV_HEADS + y) * dh
            src = list(range(base, base + dh))
            c += (src + src) if slot % 2 == 1 else (lead + src + tail)
    seg(c)
    c = []
    for y in range(NSA_KV_HEADS):
        blk = [gl0 + br * NSA_HEADS + y * NSA_GQA + g for br in range(3) for g in range(NSA_GQA)]
        c += blk + zeros(LANE - len(blk))
    seg(c)
    seg(list(range(gn0, gn0 + NSA_WIDTH)))
    seg(list(range(cq0, cq0 + MLA_Q_RANK)))
    half = MLA_ROPE // 2
    kr = list(range(kr0, kr0 + MLA_ROPE))
    pad_hi = zeros(LANE - MLA_NOPE - MLA_ROPE)
    seg(list(range(ckv0, ckv0 + MLA_KV_RANK))
        + zeros(MLA_NOPE) + kr + pad_hi
        + zeros(MLA_NOPE) + kr[half:] + kr[:half] + pad_hi)
    seg(list(range(gm0, gm0 + MLA_WIDTH)))
    return np.asarray(cols, np.int32), segs


_ODD_COLS, _ODD_SEGS = _odd_in_layout()
_ODD_DTYPES = (BF16, F32, BF16, F32, F32, F32, F32, F32)
_ODD_KVN_SEG = 2
_ODD_COL_SCALE = np.where(np.arange(len(_ODD_COLS)) < _ODD_SEGS[0][1], NSA_HEAD_DIM ** -0.5, 1.0).astype(np.float32)


def _gather_cols(w, cols, col_scale=None):
    g = jnp.where(jnp.asarray(cols >= 0)[None, :], w[:, np.maximum(cols, 0)], 0.0)
    return g if col_scale is None else g * jnp.asarray(col_scale)[None, :]


def _mla_weight_layout():
    dq = MLA_NOPE + MLA_ROPE
    half = MLA_ROPE // 2
    qa, qb, kn, vv = [], [], [], []
    for h in range(MLA_HEADS):
        nope = list(range(h * dq, h * dq + MLA_NOPE))
        rope = list(range(h * dq + MLA_NOPE, (h + 1) * dq))
        pad = [-1] * (LANE - dq)
        qa += nope + rope + pad
        qb += [-1] * MLA_NOPE + rope[half:] + rope[:half] + pad
        kv0 = h * (MLA_NOPE + MLA_V)
        kn += list(range(kv0, kv0 + MLA_NOPE)) + [-1] * (LANE - MLA_NOPE)
        vv += list(range(kv0 + MLA_NOPE, kv0 + MLA_NOPE + MLA_V))
    return tuple(np.asarray(c, np.int32) for c in (qa, qb, kn, vv))


_MLA_QA, _MLA_QB, _MLA_KN, _MLA_VV = _mla_weight_layout()


def _nsa_constants(s_len):
    n_cmp = (s_len - CMP_LEN) // CMP_STRIDE + 1
    n_sel = s_len // SEL_LEN
    cs = np.arange(LANE)[:, None] * CMP_STRIDE
    sb = np.arange(LANE)[None]
    overlap = ((cs < (sb + 1) * SEL_LEN) & (cs + CMP_LEN > sb * SEL_LEN)
               & (np.arange(LANE)[:, None] < n_cmp) & (sb < n_sel)).astype(np.float32)

    def pos_lanes(pos):
        tab = np.zeros((len(pos), LANE), np.float32)
        tab[:, NSA_POS_HI] = pos // 256
        tab[:, NSA_POS_LO] = pos % 256
        return tab

    t = np.arange(s_len)
    tab_win = pos_lanes(t)
    tab_sel = pos_lanes(t)
    tab_sel[t, t // SEL_LEN] = 1.0
    zero = np.zeros((s_len, LANE), np.float32)
    kvn_tab = np.concatenate([tab_sel, tab_sel, zero, zero, tab_win, tab_win, zero, zero], axis=1)
    cmp_tab = pos_lanes(np.arange(LANE) * CMP_STRIDE + (CMP_LEN - 1))
    return jnp.asarray(overlap, BF16), jnp.asarray(kvn_tab), jnp.asarray(cmp_tab)


_EVEN_SEGS = ((0, 512), (512, 1024), (1024, 2048), (2048, 2560))


def _even_layer(h, h_batch_major, p, layer, out_batch_major, bsz, s_len, pre_g, post_g, w_gate, w_ple,
                w_in, lam_re, lam_im, b_re, b_im, c_re, c_im, d_skip, log_step, w_glu, b_glu,
                w_dw, b_dw, ln_g, ln_b, w_pw, w_out):
    ua, ga, vb, gb = _in_proj(h, h_batch_major, pre_g, w_in.astype(BF16), _EVEN_SEGS, (F32,) * 4, True,
                              bsz, s_len)
    rows = s_len * bsz
    wb, a_re, a_im, wc = _s5_weights(lam_re, lam_im, b_re, b_im, c_re, c_im, log_step)
    ya = _s5(ua.reshape(rows, S5_WIDTH), ga.reshape(rows, S5_WIDTH), wb, a_re, a_im, wc, d_skip,
             w_glu.astype(BF16), b_glu, bsz, s_len)
    yb = _conv(vb.reshape(rows, 2 * CONV_WIDTH), gb.reshape(rows, CONV_WIDTH), w_dw, b_dw, ln_g, ln_b,
               w_pw.astype(BF16), bsz, s_len)
    return _out_proj(ya.reshape(s_len, bsz * S5_WIDTH), yb.reshape(s_len, bsz * CONV_WIDTH), True,
                     h, h_batch_major, p, layer, w_out.astype(BF16), post_g, w_gate.astype(BF16),
                     w_ple.astype(BF16), out_batch_major, bsz, s_len)


def _odd_layer(h, p, layer, out_batch_major, bsz, s_len, rope_cos, rope_sin, pre_g, post_g, w_gate, w_ple,
               w_in, pos_k, pos_v, ck_w1, ck_w2, cv_w1, cv_w2, q_norm, kv_norm, w_uq, w_ukv, w_out):
    assert s_len // SEL_LEN <= NSA_SEL_LANES and s_len // CMP_STRIDE == LANE
    overlap, kvn_tab, cmp_tab = _nsa_constants(s_len)
    w_in_p = _gather_cols(w_in, _ODD_COLS, _ODD_COL_SCALE).astype(BF16)
    q, kcv, kvn, gl, gn, cq, ckr, gm = _in_proj(h, False, pre_g, w_in_p, _ODD_SEGS, _ODD_DTYPES, False,
                                                bsz, s_len, _ODD_KVN_SEG, kvn_tab)
    n_chunks = s_len // CMP_STRIDE
    chunks = kcv.reshape(bsz, n_chunks, CMP_STRIDE, 2, NSA_KV_HEADS, NSA_HEAD_DIM)
    chunks = jnp.transpose(chunks, (0, 3, 4, 1, 2, 5)).reshape(
        bsz, 2, NSA_KV_HEADS, n_chunks, CMP_STRIDE * NSA_HEAD_DIM)
    kw2 = jnp.pad(ck_w2, ((0, 0), (NSA_D0, LANE - NSA_D0 - NSA_HEAD_DIM))).astype(BF16)
    vw2 = jnp.concatenate([cv_w2, cv_w2], axis=1).astype(BF16)
    kc, vc = _nsa_compress(chunks, pos_k.reshape(1, -1), pos_v.reshape(1, -1), ck_w1.astype(BF16), kw2,
                           cv_w1.astype(BF16), vw2, cmp_tab, bsz, n_chunks)
    y_c = _nsa_attn(q, kc, vc, kvn, gl, gn, overlap, bsz, s_len)
    qm, km, vm = _mla_proj(cq, ckr, rope_cos, rope_sin, q_norm, kv_norm,
                           _gather_cols(w_uq, _MLA_QA).astype(BF16), _gather_cols(w_uq, _MLA_QB).astype(BF16),
                           _gather_cols(w_ukv, _MLA_KN).astype(BF16), w_ukv[:, _MLA_VV].astype(BF16),
                           bsz, s_len)
    y_d = _mla_attn(qm, km, vm, gm, bsz, s_len)
    return _out_proj(y_c, y_d, False, h, False, p, layer, w_out.astype(BF16), post_g, w_gate.astype(BF16),
                     w_ple.astype(BF16), out_batch_major, bsz, s_len)


def kernel(x, p, positions, pre_norm, post_norm, ple_gate, ple_proj, ev_w_in, s5_lam_re, s5_lam_im, s5_b_re, s5_b_im, s5_c_re, s5_c_im, s5_d, s5_log_step, s5_w_glu, s5_b_glu, cv_w_dw, cv_b_dw, cv_ln_g, cv_ln_b, cv_w_pw, ev_w_out, od_w_in, nsa_pos_k, nsa_pos_v, nsa_ck_w1, nsa_ck_w2, nsa_cv_w1, nsa_cv_w2, mla_q_norm, mla_kv_norm, mla_w_uq, mla_w_ukv, od_w_out):
    bsz, s_len, _ = x.shape
    assert bsz == SUBLANE and s_len % max(TM, NSA_TK, MLA_TK) == 0
    depth = p.shape[0]
    rope_cos, rope_sin = _rope_tables(positions, bsz, s_len)
    h = x
    for i in range(depth):
        j = i // 2
        last = i == depth - 1
        if i % 2 == 0:
            h = _even_layer(h, i == 0, p, i, last, bsz, s_len, pre_norm[i], post_norm[i], ple_gate[i],
                            ple_proj[i], ev_w_in[j], s5_lam_re[j], s5_lam_im[j], s5_b_re[j], s5_b_im[j],
                            s5_c_re[j], s5_c_im[j], s5_d[j], s5_log_step[j], s5_w_glu[j], s5_b_glu[j],
                            cv_w_dw[j], cv_b_dw[j], cv_ln_g[j], cv_ln_b[j], cv_w_pw[j], ev_w_out[j])
        else:
            h = _odd_layer(h, p, i, last, bsz, s_len, rope_cos, rope_sin, pre_norm[i], post_norm[i],
                           ple_gate[i], ple_proj[i], od_w_in[j], nsa_pos_k[j], nsa_pos_v[j], nsa_ck_w1[j],
                           nsa_ck_w2[j], nsa_cv_w1[j], nsa_cv_w2[j], mla_q_norm[j], mla_kv_norm[j],
                           mla_w_uq[j], mla_w_ukv[j], od_w_out[j])
    return h
```

```python
import functools
import math

import jax
import jax.numpy as jnp
import numpy as np
from jax import lax
from jax.experimental import pallas as pl
from jax.experimental.pallas import tpu as pltpu

F32 = jnp.float32
BF16 = jnp.bfloat16

D_MODEL = 1024
DEPTH = 4
PLE_DIM = 256
EPS = 1e-6
NEG = -1e30

S5_WIDTH = 512
S5_GROUP = 16
S5_GROUPS = 32
S5_STATE = 64
S5_HALF_GROUPS = 16
S5_HALF_STATE = S5_HALF_GROUPS * S5_STATE
CONV_WIDTH = 512
CONV_K = 31

NSA_HEADS = 8
NSA_KV_HEADS = 2
NSA_GQA = 4
NSA_HEAD_DIM = 64
NSA_WIDTH = 512
CMP_LEN = 32
CMP_STRIDE = 16
CMP_HIDDEN = 256
SEL_LEN = 64
SEL_TOPN = 8
WINDOW = 512

MLA_HEADS = 8
MLA_NOPE = 64
MLA_ROPE = 32
MLA_V = 64
MLA_Q_RANK = 256
MLA_KV_RANK = 128
MLA_WIDTH = 512
ROPE_BASE = 10000.0

LANE = 128
SUBLANE = 8
VMEM_LIMIT = 56 * 1024 * 1024

TM = 256
S5_TB = 64
CONV_TB = 64
CONV_ROWS = 32
NSA_TQ = 256
NSA_TK = 512
MLA_TQ = 512
MLA_TK = 512
MLA_SUB_ROWS = 256

NSA_SEL_LANES = 32
NSA_D0 = NSA_SEL_LANES
NSA_POS_HI = NSA_D0 + NSA_HEAD_DIM
NSA_POS_LO = NSA_POS_HI + 1
MASKED = -1e30
M_INIT = -1e29
MLA_SCALE = (MLA_NOPE + MLA_ROPE) ** -0.5


def _cparams(sem):
    return pltpu.CompilerParams(dimension_semantics=sem, vmem_limit_bytes=VMEM_LIMIT)


def _dot(a, b):
    return jnp.dot(a, b, preferred_element_type=F32)


def _dot_nt(a, b):
    return lax.dot_general(a, b, (((1,), (1,)), ((), ())), preferred_element_type=F32)


def _rms(x, g):
    return x * lax.rsqrt(jnp.mean(x * x, axis=-1, keepdims=True) + EPS) * g


def _sigmoid(x):
    return 1.0 / (1.0 + jnp.exp(-x))


def _silu(x):
    return x * _sigmoid(x)


def _gelu_tanh(x):
    c = math.sqrt(2.0 / math.pi)
    return 0.5 * x * (1.0 + jnp.tanh(c * (x + 0.044715 * (x * x * x))))


def _masked_softmax(s, mask):
    s = jnp.where(mask, s, NEG)
    m = jnp.max(s, axis=-1, keepdims=True)
    e = jnp.where(mask, jnp.exp(s - m), 0.0)
    return e / jnp.maximum(jnp.sum(e, axis=-1, keepdims=True), 1e-30)


def _flash_tile(scores, v, m_ref, l_ref, acc_ref, row_slices):
    n_rep = scores[0].shape[1] // LANE
    stats = []
    for s, r in zip(scores, row_slices):
        m_old = m_ref[r, :]
        m_new = jnp.maximum(m_old, jnp.max(s, axis=-1, keepdims=True))
        p = jnp.exp(s - jnp.concatenate([m_new] * n_rep, axis=1))
        stats.append((p.astype(BF16), jnp.exp(m_old - m_new), jnp.sum(p, axis=-1, keepdims=True), m_new))
    pvs = [_dot(p, v) for p, _, _, _ in stats]
    for (_, alpha, row_sum, m_new), pv, r in zip(stats, pvs, row_slices):
        l_ref[r, :] = alpha * l_ref[r, :] + row_sum
        acc_ref[r, :] = alpha * acc_ref[r, :] + pv
        m_ref[r, :] = m_new


def _in_proj_kernel(segs, tab_seg, x_ref, g_ref, w_ref, *refs):
    tab_ref, out_refs = (None, refs) if tab_seg is None else (refs[0], refs[1:])
    xn = _rms(x_ref[...], g_ref[...]).astype(BF16)
    for n, ((c0, c1), o_ref) in enumerate(zip(segs, out_refs)):
        y = _dot(xn, w_ref[:, c0:c1])
        if n == tab_seg:
            y = y + tab_ref[...]
        o_ref[...] = y.astype(o_ref.dtype)


def _in_proj(h, h_batch_major, gain, w, segs, out_dtypes, out_time_major, bsz, s_len, tab_seg=None, tab=None):
    n_t = s_len // TM
    if h_batch_major:
        x_spec = pl.BlockSpec((None, TM, D_MODEL), lambda b, i: (b, i, 0))
    else:
        x_spec = pl.BlockSpec((TM, D_MODEL), lambda b, i: (i, b))
    out_shapes, out_specs = [], []
    for (c0, c1), dt in zip(segs, out_dtypes):
        n = c1 - c0
        if out_time_major:
            out_shapes.append(jax.ShapeDtypeStruct((s_len, bsz * n), dt))
            out_specs.append(pl.BlockSpec((TM, n), lambda b, i: (i, b)))
        else:
            out_shapes.append(jax.ShapeDtypeStruct((bsz, s_len, n), dt))
            out_specs.append(pl.BlockSpec((None, TM, n), lambda b, i: (b, i, 0)))
    in_specs = [x_spec,
                pl.BlockSpec((1, D_MODEL), lambda b, i: (0, 0)),
                pl.BlockSpec(w.shape, lambda b, i: (0, 0))]
    args = [h, gain.reshape(1, D_MODEL), w]
    if tab_seg is not None:
        in_specs.append(pl.BlockSpec((TM, tab.shape[1]), lambda b, i: (i, 0)))
        args.append(tab)
    return pl.pallas_call(
        functools.partial(_in_proj_kernel, tuple(segs), tab_seg),
        grid=(bsz, n_t),
        in_specs=in_specs,
        out_specs=out_specs,
        out_shape=out_shapes,
        compiler_params=_cparams(("parallel", "parallel")),
        name="in_proj",
    )(*args)


def _out_proj_kernel(gated, ya_ref, yb_ref, *refs):
    if gated:
        ga_ref, gb_ref, h_ref, p_ref, wo_ref, gpost_ref, wg_ref, wp_ref, o_ref = refs
        ya = (ya_ref[...] * _silu(ga_ref[...])).astype(BF16)
        yb = (yb_ref[...] * _silu(gb_ref[...])).astype(BF16)
    else:
        h_ref, p_ref, wo_ref, gpost_ref, wg_ref, wp_ref, o_ref = refs
        ya, yb = ya_ref[...], yb_ref[...]
    half = wo_ref.shape[0] // 2
    y = _dot(ya, wo_ref[0:half, :]) + _dot(yb, wo_ref[half:, :])
    h1 = h_ref[...] + _rms(y, gpost_ref[...])
    gate = _sigmoid(_dot(h1.astype(BF16), wg_ref[...]))
    o_ref[...] = h1 + gate * _dot(p_ref[...].astype(BF16), wp_ref[...])


def _out_proj(ya, yb, y_time_major, h, h_batch_major, p, layer, w_out, g_post, w_gate, w_ple,
              out_batch_major, bsz, s_len, gates=None):
    n_t = s_len // TM
    half = w_out.shape[0] // 2
    if y_time_major:
        y_spec = pl.BlockSpec((TM, half), lambda b, i: (i, b))
    else:
        y_spec = pl.BlockSpec((None, TM, half), lambda b, i: (b, i, 0))
    if h_batch_major:
        h_spec = pl.BlockSpec((None, TM, D_MODEL), lambda b, i: (b, i, 0))
    else:
        h_spec = pl.BlockSpec((TM, D_MODEL), lambda b, i: (i, b))
    if out_batch_major:
        o_shape = jax.ShapeDtypeStruct((bsz, s_len, D_MODEL), F32)
        o_spec = pl.BlockSpec((None, TM, D_MODEL), lambda b, i: (b, i, 0))
    else:
        o_shape = jax.ShapeDtypeStruct((s_len, bsz * D_MODEL), F32)
        o_spec = pl.BlockSpec((TM, D_MODEL), lambda b, i: (i, b))
    const = lambda b, i: (0, 0)
    gated = gates is not None
    return pl.pallas_call(
        functools.partial(_out_proj_kernel, gated),
        grid=(bsz, n_t),
        in_specs=[y_spec, y_spec] + ([y_spec, y_spec] if gated else []) + [
                  h_spec,
                  pl.BlockSpec((None, None, TM, PLE_DIM), lambda b, i: (layer, b, i, 0)),
                  pl.BlockSpec(w_out.shape, const),
                  pl.BlockSpec((1, D_MODEL), const),
                  pl.BlockSpec(w_gate.shape, const),
                  pl.BlockSpec(w_ple.shape, const)],
        out_specs=o_spec,
        out_shape=o_shape,
        compiler_params=_cparams(("parallel", "parallel")),
        name="out_proj",
    )(ya, yb, *(gates or ()), h, p, w_out, g_post.reshape(1, D_MODEL), w_gate, w_ple)


def _s5_kernel(u_ref, wb_ref, are_ref, aim_ref, wc_ref, d_ref, wglu_ref, bglu_ref,
               o_ref, buf_ref, st_ref):
    rows = u_ref.shape[0]
    n_steps = rows // SUBLANE
    hs = S5_HALF_STATE

    @pl.when(pl.program_id(0) == 0)
    def _():
        st_ref[...] = jnp.zeros_like(st_ref)

    u = u_ref[...]
    ub = u.astype(BF16)
    ys = []
    for hf in range(2):
        c0 = hf * 2 * hs
        buf_ref[:, c0:c0 + 2 * hs] = _dot(ub[:, hf * 256:(hf + 1) * 256], wb_ref[hf])
        a_re = jnp.broadcast_to(are_ref[hf], (SUBLANE, hs))
        a_im = jnp.broadcast_to(aim_ref[hf], (SUBLANE, hs))

        def step(t, carry, c0=c0, a_re=a_re, a_im=a_im):
            x_re, x_im = carry
            r0 = pl.multiple_of(t * SUBLANE, SUBLANE)
            b_re = buf_ref[pl.ds(r0, SUBLANE), c0:c0 + hs]
            b_im = buf_ref[pl.ds(r0, SUBLANE), c0 + hs:c0 + 2 * hs]
            n_re = a_re * x_re - a_im * x_im + b_re
            n_im = a_re * x_im + a_im * x_re + b_im
            buf_ref[pl.ds(r0, SUBLANE), c0:c0 + hs] = n_re
            buf_ref[pl.ds(r0, SUBLANE), c0 + hs:c0 + 2 * hs] = n_im
            return n_re, n_im

        x_re, x_im = lax.fori_loop(
            0, n_steps, step, (st_ref[:, c0:c0 + hs], st_ref[:, c0 + hs:c0 + 2 * hs]))
        st_ref[:, c0:c0 + hs] = x_re
        st_ref[:, c0 + hs:c0 + 2 * hs] = x_im
        ys.append(_dot(buf_ref[:, c0:c0 + 2 * hs].astype(BF16), wc_ref[hf]))
    y = jnp.concatenate(ys, axis=-1) + d_ref[...] * u
    y = _gelu_tanh(y)
    y = y * _sigmoid(_dot(y.astype(BF16), wglu_ref[...]) + bglu_ref[...])
    o_ref[...] = y.astype(o_ref.dtype)


def _s5(u, wb, a_re, a_im, wc, d_skip, w_glu, b_glu, bsz, s_len):
    rows = S5_TB * bsz
    n_blk = s_len // S5_TB
    const2 = lambda i: (0, 0)
    const3 = lambda i: (0, 0, 0)
    return pl.pallas_call(
        _s5_kernel,
        grid=(n_blk,),
        in_specs=[pl.BlockSpec((rows, S5_WIDTH), lambda i: (i, 0)),
                  pl.BlockSpec(wb.shape, const3),
                  pl.BlockSpec(a_re.shape, const3),
                  pl.BlockSpec(a_im.shape, const3),
                  pl.BlockSpec(wc.shape, const3),
                  pl.BlockSpec((1, S5_WIDTH), const2),
                  pl.BlockSpec(w_glu.shape, const2),
                  pl.BlockSpec((1, S5_WIDTH), const2)],
        out_specs=pl.BlockSpec((rows, S5_WIDTH), lambda i: (i, 0)),
        out_shape=jax.ShapeDtypeStruct((s_len * bsz, S5_WIDTH), BF16),
        scratch_shapes=[pltpu.VMEM((rows, 4 * S5_HALF_STATE), F32),
                        pltpu.VMEM((SUBLANE, 4 * S5_HALF_STATE), F32)],
        compiler_params=_cparams(("arbitrary",)),
        name="s5",
    )(u, wb, a_re, a_im, wc, d_skip.reshape(1, S5_WIDTH), w_glu, b_glu.reshape(1, S5_WIDTH))


def _s5_weights(lam_re, lam_im, b_re, b_im, c_re, c_im, log_step):
    step = jnp.exp(log_step.astype(F32))[:, None]
    lr, li = lam_re.astype(F32), lam_im.astype(F32)
    mag = jnp.exp(lr * step)
    ab_re, ab_im = mag * jnp.cos(li * step), mag * jnp.sin(li * step)
    den = lr * lr + li * li
    nr, ni = ab_re - 1.0, ab_im
    f_re, f_im = (nr * lr + ni * li) / den, (ni * lr - nr * li) / den
    br, bi = b_re.astype(F32), b_im.astype(F32)
    bb_re = f_re[..., None] * br - f_im[..., None] * bi
    bb_im = f_re[..., None] * bi + f_im[..., None] * br
    hg = S5_HALF_GROUPS
    eye = jnp.eye(hg, dtype=F32)

    def b_blockdiag(bb):
        bb = bb.reshape(2, hg, S5_STATE, S5_GROUP)
        return jnp.einsum('hgnc,gk->hgckn', bb, eye).reshape(2, hg * S5_GROUP, hg * S5_STATE)

    def c_blockdiag(cc):
        cc = cc.reshape(2, hg, S5_GROUP, S5_STATE)
        return jnp.einsum('hgcn,gk->hgnkc', cc, eye).reshape(2, hg * S5_STATE, hg * S5_GROUP)

    wb = jnp.concatenate([b_blockdiag(bb_re), b_blockdiag(bb_im)], axis=-1).astype(BF16)
    wc = jnp.concatenate([c_blockdiag(c_re.astype(F32)), -c_blockdiag(c_im.astype(F32))],
                         axis=1).astype(BF16)
    a_re = ab_re.reshape(2, 1, S5_HALF_STATE)
    a_im = ab_im.reshape(2, 1, S5_HALF_STATE)
    return wb, a_re, a_im, wc


def _conv_kernel(v_ref, wdw_ref, bdw_ref, lng_ref, lnb_ref, wpw_ref, o_ref, hbuf_ref, cbuf_ref):
    rows = v_ref.shape[0]
    halo = (CONV_K - 1) * SUBLANE

    @pl.when(pl.program_id(0) == 0)
    def _():
        hbuf_ref[0:halo, :] = jnp.zeros((halo, CONV_WIDTH), F32)

    @pl.when(pl.program_id(0) > 0)
    def _():
        hbuf_ref[0:halo, :] = hbuf_ref[rows:rows + halo, :]

    hbuf_ref[halo:halo + rows, :] = v_ref[:, 0:CONV_WIDTH] * _sigmoid(v_ref[:, CONV_WIDTH:])

    def chunk(c, carry):
        r0 = pl.multiple_of(c * CONV_ROWS, CONV_ROWS)
        acc = jnp.zeros((CONV_ROWS, CONV_WIDTH), F32)
        for k in range(CONV_K):
            acc = acc + wdw_ref[k:k + 1, :] * hbuf_ref[pl.ds(r0 + k * SUBLANE, CONV_ROWS), :]
        cbuf_ref[pl.ds(r0, CONV_ROWS), :] = acc
        return carry

    lax.fori_loop(0, rows // CONV_ROWS, chunk, 0)
    c = cbuf_ref[...] + bdw_ref[...]
    mu = jnp.mean(c, axis=-1, keepdims=True)
    var = jnp.mean(jnp.square(c - mu), axis=-1, keepdims=True)
    hn = _silu((c - mu) * lax.rsqrt(var + EPS) * lng_ref[...] + lnb_ref[...])
    y = _dot(hn.astype(BF16), wpw_ref[...])
    o_ref[...] = y.astype(o_ref.dtype)


def _conv(v, w_dw, b_dw, ln_g, ln_b, w_pw, bsz, s_len):
    rows = CONV_TB * bsz
    n_blk = s_len // CONV_TB
    halo = (CONV_K - 1) * SUBLANE
    const2 = lambda i: (0, 0)
    vec = lambda a: a.reshape(1, CONV_WIDTH)
    return pl.pallas_call(
        _conv_kernel,
        grid=(n_blk,),
        in_specs=[pl.BlockSpec((rows, 2 * CONV_WIDTH), lambda i: (i, 0)),
                  pl.BlockSpec((CONV_K, CONV_WIDTH), const2),
                  pl.BlockSpec((1, CONV_WIDTH), const2),
                  pl.BlockSpec((1, CONV_WIDTH), const2),
                  pl.BlockSpec((1, CONV_WIDTH), const2),
                  pl.BlockSpec(w_pw.shape, const2)],
        out_specs=pl.BlockSpec((rows, CONV_WIDTH), lambda i: (i, 0)),
        out_shape=jax.ShapeDtypeStruct((s_len * bsz, CONV_WIDTH), BF16),
        scratch_shapes=[pltpu.VMEM((halo + rows, CONV_WIDTH), F32),
                        pltpu.VMEM((rows, CONV_WIDTH), F32)],
        compiler_params=_cparams(("arbitrary",)),
        name="conv",
    )(v, w_dw, vec(b_dw), vec(ln_g), vec(ln_b), w_pw)


def _nsa_compress_kernel(ck_ref, cv_ref, pk_ref, pv_ref, kw1_ref, kw2_ref, vw1_ref, vw2_ref, ktab_ref,
                         ko_ref, vo_ref):
    half = CMP_STRIDE * NSA_HEAD_DIM
    n_chunks = ck_ref.shape[0]

    def compress(c_ref, pos_ref, w1_ref, w2_ref):
        c = c_ref[...]
        lo = _dot((c + pos_ref[:, 0:half]).astype(BF16), w1_ref[0:half, :])
        hi = _dot((c + pos_ref[:, half:]).astype(BF16), w1_ref[half:, :])
        hid = lo + pltpu.roll(hi, n_chunks - 1, 0)
        return _dot(_gelu_tanh(hid).astype(BF16), w2_ref[...])

    ko_ref[...] = (compress(ck_ref, pk_ref, kw1_ref, kw2_ref) + ktab_ref[...]).astype(ko_ref.dtype)
    vo_ref[...] = compress(cv_ref, pv_ref, vw1_ref, vw2_ref).astype(vo_ref.dtype)


def _nsa_compress(chunks, pos_k, pos_v, kw1, kw2, vw1, vw2, ktab, bsz, n_chunks):
    width = CMP_STRIDE * NSA_HEAD_DIM
    const2 = lambda b, y: (0, 0)
    o_shape = jax.ShapeDtypeStruct((bsz, NSA_KV_HEADS, n_chunks, LANE), BF16)
    o_spec = pl.BlockSpec((None, None, n_chunks, LANE), lambda b, y: (b, y, 0, 0))
    return pl.pallas_call(
        _nsa_compress_kernel,
        grid=(bsz, NSA_KV_HEADS),
        in_specs=[pl.BlockSpec((None, None, None, n_chunks, width), lambda b, y: (b, 0, y, 0, 0)),
                  pl.BlockSpec((None, None, None, n_chunks, width), lambda b, y: (b, 1, y, 0, 0)),
                  pl.BlockSpec(pos_k.shape, const2),
                  pl.BlockSpec(pos_v.shape, const2),
                  pl.BlockSpec(kw1.shape, const2),
                  pl.BlockSpec(kw2.shape, const2),
                  pl.BlockSpec(vw1.shape, const2),
                  pl.BlockSpec(vw2.shape, const2),
                  pl.BlockSpec(ktab.shape, const2)],
        out_specs=[o_spec, o_spec],
        out_shape=[o_shape, o_shape],
        compiler_params=_cparams(("parallel", "parallel")),
        name="nsa_compress",
    )(chunks, chunks, pos_k, pos_v, kw1, kw2, vw1, vw2, ktab)


def _nsa_attn_kernel(s_len, q_ref, kc_ref, vc_ref, ks_ref, vs_ref, kw_ref, vw_ref, gl_ref, gn_ref,
                     ov_ref, o_ref, m_ref, l_ref, acc_ref):
    tq, tk = NSA_TQ, NSA_TK
    y = pl.program_id(1)
    t0 = pl.program_id(2) * tq
    n_cmp = (s_len - CMP_LEN) // CMP_STRIDE + 1
    n_sel_blocks = s_len // SEL_LEN
    n_top = min(SEL_TOPN, n_sel_blocks)
    lane_i = lax.broadcasted_iota(jnp.int32, (tq, LANE), 1)
    tpos = t0 + lax.broadcasted_iota(jnp.int32, (tq, LANE), 0)
    slopes = [jnp.where(y == 0, 2.0 ** -(g + 1), 2.0 ** -(NSA_GQA + g + 1)).astype(F32)
              for g in range(NSA_GQA)]

    heads = range(NSA_GQA)

    def q_heads(extra):
        parts = []
        for g in heads:
            aug = jnp.where(lane_i == NSA_POS_HI, slopes[g] * 256.0,
                            jnp.where(lane_i == NSA_POS_LO, slopes[g], 0.0))
            if extra is not None:
                aug = aug + extra
            parts.append(q_ref[:, g * LANE:(g + 1) * LANE] + aug.astype(BF16))
        return parts

    mask_c = (tpos >= lane_i * CMP_STRIDE + (CMP_LEN - 1)) & (lane_i < n_cmp)
    kc, vc = kc_ref[...], vc_ref[...]
    p_c = [_masked_softmax(_dot_nt(q, kc), mask_c) for q in q_heads(None)]
    o_cmp = [_dot(p.astype(BF16), vc) for p in p_c]

    p_sum = (p_c[0] + p_c[1]) + (p_c[2] + p_c[3])
    p_hi = p_sum.astype(BF16)
    p_lo = (p_sum - p_hi.astype(F32)).astype(BF16)
    p_slc = _dot(p_hi, ov_ref[...]) + _dot(p_lo, ov_ref[...])
    cur = tpos // SEL_LEN
    forced = (lane_i == 0) | (lane_i == cur) | (lane_i == cur - 1)
    causal_blk = lane_i * SEL_LEN <= tpos
    score = jnp.where(forced, p_slc + 1e4, jnp.where(causal_blk, p_slc, -1e4))
    score_t = score.T[0:n_sel_blocks, :]
    j_idx = lax.broadcasted_iota(jnp.int32, score_t.shape, 0)
    rank = jnp.zeros(score_t.shape, F32)
    for blk in range(n_sel_blocks):
        row = score_t[blk:blk + 1, :]
        beats = (row > score_t) | ((row == score_t) & (j_idx > blk))
        rank = rank + jnp.where(beats, 1.0, 0.0)
    bias_t = jnp.where(rank < n_top, 0.0, MASKED)
    sel_bias = jnp.concatenate([bias_t, jnp.zeros((LANE - n_sel_blocks, tq), F32)], axis=0).T

    qs = q_heads(sel_bias)
    m_ref[...] = jnp.full_like(m_ref, M_INIT)
    l_ref[...] = jnp.zeros_like(l_ref)
    acc_ref[...] = jnp.zeros_like(acc_ref)

    def sel_tile(kt, causal_mask):
        k0 = pl.multiple_of(kt * tk, tk)
        k = ks_ref[pl.ds(k0, tk), :]
        v = vs_ref[pl.ds(k0, tk), :]
        scores = [_dot_nt(qs[g], k) for g in heads]
        if causal_mask:
            ok = (k0 + lax.broadcasted_iota(jnp.int32, (tq, tk), 1)
                  <= t0 + lax.broadcasted_iota(jnp.int32, (tq, tk), 0))
            scores = [jnp.where(ok, s, MASKED) for s in scores]
        _flash_tile(scores, v, m_ref, l_ref, acc_ref, [slice(g * tq, (g + 1) * tq) for g in heads])

    n_full = t0 // tk

    def full_tile(kt, carry):
        sel_tile(kt, False)
        return carry

    lax.fori_loop(0, n_full, full_tile, 0)
    sel_tile(n_full, True)

    span = min(WINDOW + tq, s_len)
    ws = pl.multiple_of(jnp.minimum(jnp.maximum(t0 - WINDOW, 0), s_len - span), tq)
    dist_w = ((t0 - ws) + lax.broadcasted_iota(jnp.int32, (tq, span), 0)
              - lax.broadcasted_iota(jnp.int32, (tq, span), 1))
    bias_w = jnp.where((dist_w >= 0) & (dist_w < WINDOW), 0.0, MASKED)
    kw = kw_ref[pl.ds(ws, span), :]
    vw = vw_ref[pl.ds(ws, span), :]
    s_w = [_dot_nt(qs[g], kw) + bias_w for g in heads]
    e_w = [jnp.exp(s - jnp.max(s, axis=-1, keepdims=True)) for s in s_w]
    sum_w = [jnp.maximum(jnp.sum(e, axis=-1, keepdims=True), 1e-30) for e in e_w]
    o_win = [_dot(e.astype(BF16), vw) / d for e, d in zip(e_w, sum_w)]

    gates = _sigmoid(gl_ref[...])
    outs = []
    for g in heads:
        r = slice(g * tq, (g + 1) * tq)
        o_sel = acc_ref[r, :] / jnp.maximum(l_ref[r, :], 1e-30)
        outs.append(gates[:, g:g + 1] * o_cmp[g]
                    + gates[:, NSA_GQA + g:NSA_GQA + g + 1] * o_sel
                    + gates[:, 2 * NSA_GQA + g:2 * NSA_GQA + g + 1] * o_win[g])
    lo = lane_i < NSA_HEAD_DIM
    o = jnp.concatenate([jnp.where(lo, outs[0], outs[1]), jnp.where(lo, outs[2], outs[3])], axis=-1)
    o_ref[...] = (o * _silu(gn_ref[...])).astype(o_ref.dtype)


def _nsa_attn(q, kc, vc, kvn, gl, gn, overlap, bsz, s_len):
    tq = NSA_TQ
    n_chunks = kc.shape[2]
    rows = NSA_GQA * tq
    kv_spec = lambda col: pl.BlockSpec((None, s_len, LANE), lambda b, y, i, col=col: (b, 0, col + y))
    c_spec = pl.BlockSpec((None, None, n_chunks, LANE), lambda b, y, i: (b, y, 0, 0))
    return pl.pallas_call(
        functools.partial(_nsa_attn_kernel, s_len),
        grid=(bsz, NSA_KV_HEADS, s_len // tq),
        in_specs=[pl.BlockSpec((None, tq, NSA_GQA * LANE), lambda b, y, i: (b, i, y)),
                  c_spec, c_spec,
                  kv_spec(0), kv_spec(2), kv_spec(4), kv_spec(6),
                  pl.BlockSpec((None, tq, LANE), lambda b, y, i: (b, i, y)),
                  pl.BlockSpec((None, tq, NSA_GQA * NSA_HEAD_DIM), lambda b, y, i: (b, i, y)),
                  pl.BlockSpec(overlap.shape, lambda b, y, i: (0, 0))],
        out_specs=pl.BlockSpec((None, tq, NSA_GQA * NSA_HEAD_DIM), lambda b, y, i: (b, i, y)),
        out_shape=jax.ShapeDtypeStruct((bsz, s_len, NSA_WIDTH), BF16),
        scratch_shapes=[pltpu.VMEM((rows, LANE), F32)] * 3,
        compiler_params=_cparams(("parallel", "parallel", "parallel")),
        name="nsa_attn",
    )(q, kc, vc, kvn, kvn, kvn, kvn, gl, gn, overlap)


def _rope_table_kernel(pos_ref, freq_ref, cos_ref, sin_ref):
    ang = pos_ref[...].astype(F32) * freq_ref[...]
    lane = lax.broadcasted_iota(jnp.int32, ang.shape, 1)
    half = MLA_ROPE // 2
    first = (lane >= MLA_NOPE) & (lane < MLA_NOPE + half)
    second = (lane >= MLA_NOPE + half) & (lane < MLA_NOPE + MLA_ROPE)
    cos_ref[...] = jnp.where(first | second, jnp.cos(ang), jnp.where(lane < MLA_NOPE, 1.0, 0.0))
    sn = jnp.sin(ang)
    sin_ref[...] = jnp.where(first, -sn, jnp.where(second, sn, 0.0))


def _rope_tables(positions, bsz, s_len):
    half = MLA_ROPE // 2
    freqs = ROPE_BASE ** (-jnp.arange(half, dtype=F32) / half)
    freq_lane = jnp.concatenate([jnp.zeros((MLA_NOPE,), F32), freqs, freqs,
                                 jnp.zeros((LANE - MLA_NOPE - MLA_ROPE,), F32)]).reshape(1, LANE)
    spec = pl.BlockSpec((None, TM, LANE), lambda b, i: (b, i, 0))
    shape = jax.ShapeDtypeStruct((bsz, s_len, LANE), F32)
    return pl.pallas_call(
        _rope_table_kernel,
        grid=(bsz, s_len // TM),
        in_specs=[pl.BlockSpec((None, TM, 1), lambda b, i: (b, i, 0)),
                  pl.BlockSpec((1, LANE), lambda b, i: (0, 0))],
        out_specs=[spec, spec],
        out_shape=[shape, shape],
        compiler_params=_cparams(("parallel", "parallel")),
        name="rope_tables",
    )(positions.reshape(bsz, s_len, 1), freq_lane)


def _mla_proj_kernel(cq_ref, ckr_ref, cos_ref, sin_ref, qn_ref, kvn_ref, wqa_ref, wqb_ref, wk_ref, wv_ref,
                     q_ref, k_ref, v_ref):
    cq = _rms(cq_ref[...], qn_ref[...]).astype(BF16)
    ckv = _rms(ckr_ref[:, 0:MLA_KV_RANK], kvn_ref[...]).astype(BF16)
    cosm = cos_ref[...]
    sinm = sin_ref[...]
    kr = ckr_ref[:, MLA_KV_RANK:MLA_KV_RANK + LANE] * cosm + ckr_ref[:, MLA_KV_RANK + LANE:] * sinm
    v_ref[...] = _dot(ckv, wv_ref[...]).astype(v_ref.dtype)
    for h in range(MLA_HEADS):
        c = slice(h * LANE, (h + 1) * LANE)
        q = _dot(cq, wqa_ref[:, c]) * cosm + _dot(cq, wqb_ref[:, c]) * sinm
        q_ref[:, c] = (q * MLA_SCALE).astype(q_ref.dtype)
        k_ref[:, c] = (_dot(ckv, wk_ref[:, c]) + kr).astype(k_ref.dtype)


def _mla_proj(cq, ckr, cosm, sinm, q_norm, kv_norm, wqa, wqb, wk, wv, bsz, s_len):
    const = lambda b, i: (0, 0)
    tok = lambda n: pl.BlockSpec((None, TM, n), lambda b, i: (b, i, 0))
    return pl.pallas_call(
        _mla_proj_kernel,
        grid=(bsz, s_len // TM),
        in_specs=[tok(MLA_Q_RANK), tok(MLA_KV_RANK + 2 * LANE), tok(LANE), tok(LANE),
                  pl.BlockSpec((1, MLA_Q_RANK), const), pl.BlockSpec((1, MLA_KV_RANK), const),
                  pl.BlockSpec(wqa.shape, const), pl.BlockSpec(wqb.shape, const),
                  pl.BlockSpec(wk.shape, const), pl.BlockSpec(wv.shape, const)],
        out_specs=[tok(MLA_HEADS * LANE), tok(MLA_HEADS * LANE), tok(MLA_WIDTH)],
        out_shape=[jax.ShapeDtypeStruct((bsz, s_len, MLA_HEADS * LANE), BF16),
                   jax.ShapeDtypeStruct((bsz, s_len, MLA_HEADS * LANE), BF16),
                   jax.ShapeDtypeStruct((bsz, s_len, MLA_WIDTH), BF16)],
        compiler_params=_cparams(("parallel", "parallel")),
        name="mla_proj",
    )(cq, ckr, cosm, sinm, q_norm.reshape(1, MLA_Q_RANK), kv_norm.reshape(1, MLA_KV_RANK), wqa, wqb, wk, wv)


def _mla_attn_kernel(q_ref, k_ref, v_ref, gm_ref, o_ref, m_ref, l_ref, acc_ref):
    tq, tk = MLA_TQ, MLA_TK
    t0 = pl.program_id(2) * tq
    n_full = t0 // tk
    sub = MLA_SUB_ROWS
    n_sub = tq // sub
    m_ref[...] = jnp.full_like(m_ref, M_INIT)
    l_ref[...] = jnp.zeros_like(l_ref)
    acc_ref[...] = jnp.zeros_like(acc_ref)

    def tile(kt, causal_mask):
        k0 = pl.multiple_of(kt * tk, tk)
        v = v_ref[pl.ds(k0, tk), :]
        scores, row_slices = [], []
        for hh in range(2):
            k = k_ref[pl.ds(k0, tk), hh * LANE:(hh + 1) * LANE]
            for rb in range(n_sub):
                s = _dot_nt(q_ref[rb * sub:(rb + 1) * sub, hh * LANE:(hh + 1) * LANE], k)
                if causal_mask:
                    ok = (k0 + lax.broadcasted_iota(jnp.int32, (sub, tk), 1)
                          <= t0 + rb * sub + lax.broadcasted_iota(jnp.int32, (sub, tk), 0))
                    s = jnp.where(ok, s, MASKED)
                scores.append(s)
                row_slices.append(slice(hh * tq + rb * sub, hh * tq + (rb + 1) * sub))
        _flash_tile(scores, v, m_ref, l_ref, acc_ref, row_slices)

    def full_tile(kt, carry):
        tile(kt, False)
        return carry

    lax.fori_loop(0, n_full, full_tile, 0)
    tile(n_full, True)
    o_all = acc_ref[...] / jnp.maximum(l_ref[...], 1e-30)
    lane = lax.broadcasted_iota(jnp.int32, (tq, LANE), 1)
    o = jnp.where(lane < MLA_V, o_all[0:tq], o_all[tq:2 * tq])
    o_ref[...] = (o * _silu(gm_ref[...])).astype(o_ref.dtype)


def _mla_attn(q, k, v, gm, bsz, s_len):
    tq = MLA_TQ
    return pl.pallas_call(
        _mla_attn_kernel,
        grid=(bsz, MLA_HEADS // 2, s_len // tq),
        in_specs=[pl.BlockSpec((None, tq, 2 * LANE), lambda b, hp, i: (b, i, hp)),
                  pl.BlockSpec((None, s_len, 2 * LANE), lambda b, hp, i: (b, 0, hp)),
                  pl.BlockSpec((None, s_len, LANE), lambda b, hp, i: (b, 0, hp)),
                  pl.BlockSpec((None, tq, LANE), lambda b, hp, i: (b, i, hp))],
        out_specs=pl.BlockSpec((None, tq, LANE), lambda b, hp, i: (b, i, hp)),
        out_shape=jax.ShapeDtypeStruct((bsz, s_len, MLA_WIDTH), BF16),
        scratch_shapes=[pltpu.VMEM((2 * tq, LANE), F32)] * 3,
        compiler_params=_cparams(("parallel", "parallel", "parallel")),
        name="mla_attn",
    )(q, k, v, gm)


def _odd_in_layout():
    off = np.cumsum([0, NSA_WIDTH, 6 * NSA_KV_HEADS * NSA_HEAD_DIM, 3 * NSA_HEADS, NSA_WIDTH,
                     MLA_Q_RANK, MLA_KV_RANK, MLA_ROPE, MLA_WIDTH])
    q0, kv0, gl0, gn0, cq0, ckv0, kr0, gm0 = off[:8]
    dh = NSA_HEAD_DIM
    zeros = lambda n: [-1] * n
    cols, segs, start = [], [], 0

    def seg(c):
        nonlocal start
        cols.extend(c)
        segs.append((start, start + len(c)))
        start += len(c)

    lead, tail = zeros(NSA_D0), zeros(LANE - NSA_D0 - dh)
    c = []
    for h in range(NSA_HEADS):
        c += lead + list(range(q0 + h * dh, q0 + (h + 1) * dh)) + tail
    seg(c)
    seg(list(range(kv0, kv0 + 2 * NSA_KV_HEADS * dh)))
    c = []
    for slot in range(2, 6):
        for y in range(NSA_KV_HEADS):
            base = kv0 + (slot * NSA_KV_HEADS + y) * dh
            src = list(range(base, base + dh))
            c += (src + src) if slot % 2 == 1 else (lead + src + tail)
    seg(c)
    c = []
    for y in range(NSA_KV_HEADS):
        blk = [gl0 + br * NSA_HEADS + y * NSA_GQA + g for br in range(3) for g in range(NSA_GQA)]
        c += blk + zeros(LANE - len(blk))
    seg(c)
    seg(list(range(gn0, gn0 + NSA_WIDTH)))
    seg(list(range(cq0, cq0 + MLA_Q_RANK)))
    half = MLA_ROPE // 2
    kr = list(range(kr0, kr0 + MLA_ROPE))
    pad_hi = zeros(LANE - MLA_NOPE - MLA_ROPE)
    seg(list(range(ckv0, ckv0 + MLA_KV_RANK))
        + zeros(MLA_NOPE) + kr + pad_hi
        + zeros(MLA_NOPE) + kr[half:] + kr[:half] + pad_hi)
    seg(list(range(gm0, gm0 + MLA_WIDTH)))
    return np.asarray(cols, np.int32), segs


_ODD_COLS, _ODD_SEGS = _odd_in_layout()
_ODD_DTYPES = (BF16, F32, BF16, F32, F32, F32, F32, F32)
_ODD_KVN_SEG = 2
_ODD_COL_SCALE = np.where(np.arange(len(_ODD_COLS)) < _ODD_SEGS[0][1], NSA_HEAD_DIM ** -0.5, 1.0).astype(np.float32)


def _gather_cols(w, cols, col_scale=None):
    cols = np.asarray(cols)
    same_run = ((cols[1:] < 0) & (cols[:-1] < 0)) | ((cols[1:] >= 0) & (cols[:-1] >= 0) & (np.diff(cols) == 1))
    breaks = np.flatnonzero(~same_run) + 1
    parts = []
    for run in np.split(cols, breaks):
        if run[0] < 0:
            parts.append(jnp.zeros((w.shape[0], len(run)), w.dtype))
        else:
            parts.append(w[:, int(run[0]):int(run[0]) + len(run)])
    g = jnp.concatenate(parts, axis=1)
    return g if col_scale is None else g * jnp.asarray(col_scale)[None, :]


def _mla_weight_layout():
    dq = MLA_NOPE + MLA_ROPE
    half = MLA_ROPE // 2
    qa, qb, kn, vv = [], [], [], []
    for h in range(MLA_HEADS):
        nope = list(range(h * dq, h * dq + MLA_NOPE))
        rope = list(range(h * dq + MLA_NOPE, (h + 1) * dq))
        pad = [-1] * (LANE - dq)
        qa += nope + rope + pad
        qb += [-1] * MLA_NOPE + rope[half:] + rope[:half] + pad
        kv0 = h * (MLA_NOPE + MLA_V)
        kn += list(range(kv0, kv0 + MLA_NOPE)) + [-1] * (LANE - MLA_NOPE)
        vv += list(range(kv0 + MLA_NOPE, kv0 + MLA_NOPE + MLA_V))
    return tuple(np.asarray(c, np.int32) for c in (qa, qb, kn, vv))


_MLA_QA, _MLA_QB, _MLA_KN, _MLA_VV = _mla_weight_layout()


def _nsa_constants(s_len):
    n_cmp = (s_len - CMP_LEN) // CMP_STRIDE + 1
    n_sel = s_len // SEL_LEN
    cs = np.arange(LANE)[:, None] * CMP_STRIDE
    sb = np.arange(LANE)[None]
    overlap = ((cs < (sb + 1) * SEL_LEN) & (cs + CMP_LEN > sb * SEL_LEN)
               & (np.arange(LANE)[:, None] < n_cmp) & (sb < n_sel)).astype(np.float32)

    def pos_lanes(pos):
        tab = np.zeros((len(pos), LANE), np.float32)
        tab[:, NSA_POS_HI] = pos // 256
        tab[:, NSA_POS_LO] = pos % 256
        return tab

    t = np.arange(s_len)
    tab_win = pos_lanes(t)
    tab_sel = pos_lanes(t)
    tab_sel[t, t // SEL_LEN] = 1.0
    zero = np.zeros((s_len, LANE), np.float32)
    kvn_tab = np.concatenate([tab_sel, tab_sel, zero, zero, tab_win, tab_win, zero, zero], axis=1)
    cmp_tab = pos_lanes(np.arange(LANE) * CMP_STRIDE + (CMP_LEN - 1))
    return jnp.asarray(overlap, BF16), jnp.asarray(kvn_tab), jnp.asarray(cmp_tab)


_EVEN_SEGS = ((0, 512), (512, 1024), (1024, 2048), (2048, 2560))


def _even_layer(h, h_batch_major, p, layer, out_batch_major, bsz, s_len, pre_g, post_g, w_gate, w_ple,
                w_in, lam_re, lam_im, b_re, b_im, c_re, c_im, d_skip, log_step, w_glu, b_glu,
                w_dw, b_dw, ln_g, ln_b, w_pw, w_out):
    ua, ga, vb, gb = _in_proj(h, h_batch_major, pre_g, w_in.astype(BF16), _EVEN_SEGS, (F32,) * 4, True,
                              bsz, s_len)
    rows = s_len * bsz
    wb, a_re, a_im, wc = _s5_weights(lam_re, lam_im, b_re, b_im, c_re, c_im, log_step)
    ya = _s5(ua.reshape(rows, S5_WIDTH), wb, a_re, a_im, wc, d_skip, w_glu.astype(BF16), b_glu, bsz, s_len)
    yb = _conv(vb.reshape(rows, 2 * CONV_WIDTH), w_dw, b_dw, ln_g, ln_b, w_pw.astype(BF16), bsz, s_len)
    return _out_proj(ya.reshape(s_len, bsz * S5_WIDTH), yb.reshape(s_len, bsz * CONV_WIDTH), True,
                     h, h_batch_major, p, layer, w_out.astype(BF16), post_g, w_gate.astype(BF16),
                     w_ple.astype(BF16), out_batch_major, bsz, s_len, gates=(ga, gb))


def _odd_layer(h, p, layer, out_batch_major, bsz, s_len, rope_cos, rope_sin, pre_g, post_g, w_gate, w_ple,
               w_in, pos_k, pos_v, ck_w1, ck_w2, cv_w1, cv_w2, q_norm, kv_norm, w_uq, w_ukv, w_out):
    assert s_len // SEL_LEN <= NSA_SEL_LANES and s_len // CMP_STRIDE == LANE
    overlap, kvn_tab, cmp_tab = _nsa_constants(s_len)
    w_in_p = _gather_cols(w_in, _ODD_COLS, _ODD_COL_SCALE).astype(BF16)
    q, kcv, kvn, gl, gn, cq, ckr, gm = _in_proj(h, False, pre_g, w_in_p, _ODD_SEGS, _ODD_DTYPES, False,
                                                bsz, s_len, _ODD_KVN_SEG, kvn_tab)
    n_chunks = s_len // CMP_STRIDE
    chunks = kcv.reshape(bsz, n_chunks, CMP_STRIDE, 2, NSA_KV_HEADS, NSA_HEAD_DIM)
    chunks = jnp.transpose(chunks, (0, 3, 4, 1, 2, 5)).reshape(
        bsz, 2, NSA_KV_HEADS, n_chunks, CMP_STRIDE * NSA_HEAD_DIM)
    kw2 = jnp.pad(ck_w2, ((0, 0), (NSA_D0, LANE - NSA_D0 - NSA_HEAD_DIM))).astype(BF16)
    vw2 = jnp.concatenate([cv_w2, cv_w2], axis=1).astype(BF16)
    kc, vc = _nsa_compress(chunks, pos_k.reshape(1, -1), pos_v.reshape(1, -1), ck_w1.astype(BF16), kw2,
                           cv_w1.astype(BF16), vw2, cmp_tab, bsz, n_chunks)
    y_c = _nsa_attn(q, kc, vc, kvn, gl, gn, overlap, bsz, s_len)
    qm, km, vm = _mla_proj(cq, ckr, rope_cos, rope_sin, q_norm, kv_norm,
                           _gather_cols(w_uq, _MLA_QA).astype(BF16), _gather_cols(w_uq, _MLA_QB).astype(BF16),
                           _gather_cols(w_ukv, _MLA_KN).astype(BF16), _gather_cols(w_ukv, _MLA_VV).astype(BF16),
                           bsz, s_len)
    y_d = _mla_attn(qm, km, vm, gm, bsz, s_len)
    return _out_proj(y_c, y_d, False, h, False, p, layer, w_out.astype(BF16), post_g, w_gate.astype(BF16),
                     w_ple.astype(BF16), out_batch_major, bsz, s_len)


def kernel(x, p, positions, pre_norm, post_norm, ple_gate, ple_proj, ev_w_in, s5_lam_re, s5_lam_im, s5_b_re, s5_b_im, s5_c_re, s5_c_im, s5_d, s5_log_step, s5_w_glu, s5_b_glu, cv_w_dw, cv_b_dw, cv_ln_g, cv_ln_b, cv_w_pw, ev_w_out, od_w_in, nsa_pos_k, nsa_pos_v, nsa_ck_w1, nsa_ck_w2, nsa_cv_w1, nsa_cv_w2, mla_q_norm, mla_kv_norm, mla_w_uq, mla_w_ukv, od_w_out):
    bsz, s_len, _ = x.shape
    assert bsz == SUBLANE and s_len % max(TM, NSA_TK, MLA_TK) == 0
    depth = p.shape[0]
    rope_cos, rope_sin = _rope_tables(positions, bsz, s_len)
    h = x
    for i in range(depth):
        j = i // 2
        last = i == depth - 1
        if i % 2 == 0:
            h = _even_layer(h, i == 0, p, i, last, bsz, s_len, pre_norm[i], post_norm[i], ple_gate[i],
                            ple_proj[i], ev_w_in[j], s5_lam_re[j], s5_lam_im[j], s5_b_re[j], s5_b_im[j],
                            s5_c_re[j], s5_c_im[j], s5_d[j], s5_log_step[j], s5_w_glu[j], s5_b_glu[j],
                            cv_w_dw[j], cv_b_dw[j], cv_ln_g[j], cv_ln_b[j], cv_w_pw[j], ev_w_out[j])
        else:
            h = _odd_layer(h, p, i, last, bsz, s_len, rope_cos, rope_sin, pre_norm[i], post_norm[i],
                           ple_gate[i], ple_proj[i], od_w_in[j], nsa_pos_k[j], nsa_pos_v[j], nsa_ck_w1[j],
                           nsa_ck_w2[j], nsa_cv_w1[j], nsa_cv_w2[j], mla_q_norm[j], mla_kv_norm[j],
                           mla_w_uq[j], mla_w_ukv[j], od_w_out[j])
    return h
```

```python
import functools
import math

import jax
import jax.numpy as jnp
import numpy as np
from jax import lax
from jax.experimental import pallas as pl
from jax.experimental.pallas import tpu as pltpu

F32 = jnp.float32
BF16 = jnp.bfloat16

D_MODEL = 1024
DEPTH = 4
PLE_DIM = 256
EPS = 1e-6
NEG = -1e30

S5_WIDTH = 512
S5_GROUP = 16
S5_GROUPS = 32
S5_STATE = 64
S5_HALF_GROUPS = 16
S5_HALF_STATE = S5_HALF_GROUPS * S5_STATE
CONV_WIDTH = 512
CONV_K = 31

NSA_HEADS = 8
NSA_KV_HEADS = 2
NSA_GQA = 4
NSA_HEAD_DIM = 64
NSA_WIDTH = 512
CMP_LEN = 32
CMP_STRIDE = 16
CMP_HIDDEN = 256
SEL_LEN = 64
SEL_TOPN = 8
WINDOW = 512

MLA_HEADS = 8
MLA_NOPE = 64
MLA_ROPE = 32
MLA_V = 64
MLA_Q_RANK = 256
MLA_KV_RANK = 128
MLA_WIDTH = 512
ROPE_BASE = 10000.0

LANE = 128
SUBLANE = 8
VMEM_LIMIT = 56 * 1024 * 1024

TM = 256
S5_TB = 64
CONV_TB = 64
CONV_ROWS = 32
NSA_TQ = 256
NSA_TK = 512
MLA_TQ = 512
MLA_TK = 512
MLA_SUB_ROWS = 256

NSA_SEL_LANES = 32
NSA_D0 = NSA_SEL_LANES
NSA_POS_HI = NSA_D0 + NSA_HEAD_DIM
NSA_POS_LO = NSA_POS_HI + 1
MASKED = -1e30
M_INIT = -1e29
MLA_SCALE = (MLA_NOPE + MLA_ROPE) ** -0.5


def _cparams(sem):
    return pltpu.CompilerParams(dimension_semantics=sem, vmem_limit_bytes=VMEM_LIMIT)


def _dot(a, b):
    return jnp.dot(a, b, preferred_element_type=F32)


def _dot_nt(a, b):
    return lax.dot_general(a, b, (((1,), (1,)), ((), ())), preferred_element_type=F32)


def _rms(x, g):
    return x * lax.rsqrt(jnp.mean(x * x, axis=-1, keepdims=True) + EPS) * g


def _sigmoid(x):
    return 1.0 / (1.0 + jnp.exp(-x))


def _silu(x):
    return x * _sigmoid(x)


def _gelu_tanh(x):
    c = math.sqrt(2.0 / math.pi)
    return 0.5 * x * (1.0 + jnp.tanh(c * (x + 0.044715 * (x * x * x))))


def _masked_softmax(s, mask):
    s = jnp.where(mask, s, NEG)
    m = jnp.max(s, axis=-1, keepdims=True)
    e = jnp.where(mask, jnp.exp(s - m), 0.0)
    return e / jnp.maximum(jnp.sum(e, axis=-1, keepdims=True), 1e-30)


def _flash_tile(scores, v, m_ref, l_ref, acc_ref, row_slices):
    n_rep = scores[0].shape[1] // LANE
    stats = []
    for s, r in zip(scores, row_slices):
        m_old = m_ref[r, :]
        m_new = jnp.maximum(m_old, jnp.max(s, axis=-1, keepdims=True))
        p = jnp.exp(s - jnp.concatenate([m_new] * n_rep, axis=1))
        stats.append((p.astype(BF16), jnp.exp(m_old - m_new), jnp.sum(p, axis=-1, keepdims=True), m_new))
    pvs = [_dot(p, v) for p, _, _, _ in stats]
    for (_, alpha, row_sum, m_new), pv, r in zip(stats, pvs, row_slices):
        l_ref[r, :] = alpha * l_ref[r, :] + row_sum
        acc_ref[r, :] = alpha * acc_ref[r, :] + pv
        m_ref[r, :] = m_new


def _in_proj_kernel(segs, tab_seg, x_ref, g_ref, w_ref, *refs):
    tab_ref, out_refs = (None, refs) if tab_seg is None else (refs[0], refs[1:])
    xn = _rms(x_ref[...], g_ref[...]).astype(BF16)
    for n, ((c0, c1), o_ref) in enumerate(zip(segs, out_refs)):
        y = _dot(xn, w_ref[:, c0:c1])
        if n == tab_seg:
            y = y + tab_ref[...]
        o_ref[...] = y.astype(o_ref.dtype)


def _in_proj(h, h_batch_major, gain, w, segs, out_dtypes, out_time_major, bsz, s_len, tab_seg=None, tab=None):
    n_t = s_len // TM
    if h_batch_major:
        x_spec = pl.BlockSpec((None, TM, D_MODEL), lambda b, i: (b, i, 0))
    else:
        x_spec = pl.BlockSpec((TM, D_MODEL), lambda b, i: (i, b))
    out_shapes, out_specs = [], []
    for (c0, c1), dt in zip(segs, out_dtypes):
        n = c1 - c0
        if out_time_major:
            out_shapes.append(jax.ShapeDtypeStruct((s_len, bsz * n), dt))
            out_specs.append(pl.BlockSpec((TM, n), lambda b, i: (i, b)))
        else:
            out_shapes.append(jax.ShapeDtypeStruct((bsz, s_len, n), dt))
            out_specs.append(pl.BlockSpec((None, TM, n), lambda b, i: (b, i, 0)))
    in_specs = [x_spec,
                pl.BlockSpec((1, D_MODEL), lambda b, i: (0, 0)),
                pl.BlockSpec(w.shape, lambda b, i: (0, 0))]
    args = [h, gain.reshape(1, D_MODEL), w]
    if tab_seg is not None:
        in_specs.append(pl.BlockSpec((TM, tab.shape[1]), lambda b, i: (i, 0)))
        args.append(tab)
    return pl.pallas_call(
        functools.partial(_in_proj_kernel, tuple(segs), tab_seg),
        grid=(bsz, n_t),
        in_specs=in_specs,
        out_specs=out_specs,
        out_shape=out_shapes,
        compiler_params=_cparams(("parallel", "parallel")),
        name="in_proj",
    )(*args)


def _out_proj_kernel(gated, ya_ref, yb_ref, *refs):
    if gated:
        ga_ref, gb_ref, h_ref, p_ref, wo_ref, gpost_ref, wg_ref, wp_ref, o_ref = refs
        ya = (ya_ref[...] * _silu(ga_ref[...])).astype(BF16)
        yb = (yb_ref[...] * _silu(gb_ref[...])).astype(BF16)
    else:
        h_ref, p_ref, wo_ref, gpost_ref, wg_ref, wp_ref, o_ref = refs
        ya, yb = ya_ref[...], yb_ref[...]
    half = wo_ref.shape[0] // 2
    y = _dot(ya, wo_ref[0:half, :]) + _dot(yb, wo_ref[half:, :])
    h1 = h_ref[...] + _rms(y, gpost_ref[...])
    gate = _sigmoid(_dot(h1.astype(BF16), wg_ref[...]))
    o_ref[...] = h1 + gate * _dot(p_ref[...].astype(BF16), wp_ref[...])


def _out_proj(ya, yb, y_time_major, h, h_batch_major, p, layer, w_out, g_post, w_gate, w_ple,
              out_batch_major, bsz, s_len, gates=None):
    n_t = s_len // TM
    half = w_out.shape[0] // 2
    if y_time_major:
        y_spec = pl.BlockSpec((TM, half), lambda b, i: (i, b))
    else:
        y_spec = pl.BlockSpec((None, TM, half), lambda b, i: (b, i, 0))
    if h_batch_major:
        h_spec = pl.BlockSpec((None, TM, D_MODEL), lambda b, i: (b, i, 0))
    else:
        h_spec = pl.BlockSpec((TM, D_MODEL), lambda b, i: (i, b))
    if out_batch_major:
        o_shape = jax.ShapeDtypeStruct((bsz, s_len, D_MODEL), F32)
        o_spec = pl.BlockSpec((None, TM, D_MODEL), lambda b, i: (b, i, 0))
    else:
        o_shape = jax.ShapeDtypeStruct((s_len, bsz * D_MODEL), F32)
        o_spec = pl.BlockSpec((TM, D_MODEL), lambda b, i: (i, b))
    const = lambda b, i: (0, 0)
    gated = gates is not None
    return pl.pallas_call(
        functools.partial(_out_proj_kernel, gated),
        grid=(bsz, n_t),
        in_specs=[y_spec, y_spec] + ([y_spec, y_spec] if gated else []) + [
                  h_spec,
                  pl.BlockSpec((None, None, TM, PLE_DIM), lambda b, i: (layer, b, i, 0)),
                  pl.BlockSpec(w_out.shape, const),
                  pl.BlockSpec((1, D_MODEL), const),
                  pl.BlockSpec(w_gate.shape, const),
                  pl.BlockSpec(w_ple.shape, const)],
        out_specs=o_spec,
        out_shape=o_shape,
        compiler_params=_cparams(("parallel", "parallel")),
        name="out_proj",
    )(ya, yb, *(gates or ()), h, p, w_out, g_post.reshape(1, D_MODEL), w_gate, w_ple)


def _s5_kernel(u_ref, wb_ref, are_ref, aim_ref, wc_ref, d_ref, wglu_ref, bglu_ref,
               o_ref, buf_ref, st_ref):
    rows = u_ref.shape[0]
    n_steps = rows // SUBLANE
    hs = S5_HALF_STATE

    @pl.when(pl.program_id(0) == 0)
    def _():
        st_ref[...] = jnp.zeros_like(st_ref)

    u = u_ref[...]
    ub = u.astype(BF16)
    ys = []
    for hf in range(2):
        c0 = hf * 2 * hs
        buf_ref[:, c0:c0 + 2 * hs] = _dot(ub[:, hf * 256:(hf + 1) * 256], wb_ref[hf])
        a_re = jnp.broadcast_to(are_ref[hf], (SUBLANE, hs))
        a_im = jnp.broadcast_to(aim_ref[hf], (SUBLANE, hs))

        def step(t, carry, c0=c0, a_re=a_re, a_im=a_im):
            x_re, x_im = carry
            r0 = pl.multiple_of(t * SUBLANE, SUBLANE)
            b_re = buf_ref[pl.ds(r0, SUBLANE), c0:c0 + hs]
            b_im = buf_ref[pl.ds(r0, SUBLANE), c0 + hs:c0 + 2 * hs]
            n_re = a_re * x_re - a_im * x_im + b_re
            n_im = a_re * x_im + a_im * x_re + b_im
            buf_ref[pl.ds(r0, SUBLANE), c0:c0 + hs] = n_re
            buf_ref[pl.ds(r0, SUBLANE), c0 + hs:c0 + 2 * hs] = n_im
            return n_re, n_im

        x_re, x_im = lax.fori_loop(
            0, n_steps, step, (st_ref[:, c0:c0 + hs], st_ref[:, c0 + hs:c0 + 2 * hs]))
        st_ref[:, c0:c0 + hs] = x_re
        st_ref[:, c0 + hs:c0 + 2 * hs] = x_im
        ys.append(_dot(buf_ref[:, c0:c0 + 2 * hs].astype(BF16), wc_ref[hf]))
    y = jnp.concatenate(ys, axis=-1) + d_ref[...] * u
    y = _gelu_tanh(y)
    y = y * _sigmoid(_dot(y.astype(BF16), wglu_ref[...]) + bglu_ref[...])
    o_ref[...] = y.astype(o_ref.dtype)


def _s5(u, wb, a_re, a_im, wc, d_skip, w_glu, b_glu, bsz, s_len):
    rows = S5_TB * bsz
    n_blk = s_len // S5_TB
    const2 = lambda i: (0, 0)
    const3 = lambda i: (0, 0, 0)
    return pl.pallas_call(
        _s5_kernel,
        grid=(n_blk,),
        in_specs=[pl.BlockSpec((rows, S5_WIDTH), lambda i: (i, 0)),
                  pl.BlockSpec(wb.shape, const3),
                  pl.BlockSpec(a_re.shape, const3),
                  pl.BlockSpec(a_im.shape, const3),
                  pl.BlockSpec(wc.shape, const3),
                  pl.BlockSpec((1, S5_WIDTH), const2),
                  pl.BlockSpec(w_glu.shape, const2),
                  pl.BlockSpec((1, S5_WIDTH), const2)],
        out_specs=pl.BlockSpec((rows, S5_WIDTH), lambda i: (i, 0)),
        out_shape=jax.ShapeDtypeStruct((s_len * bsz, S5_WIDTH), BF16),
        scratch_shapes=[pltpu.VMEM((rows, 4 * S5_HALF_STATE), F32),
                        pltpu.VMEM((SUBLANE, 4 * S5_HALF_STATE), F32)],
        compiler_params=_cparams(("arbitrary",)),
        name="s5",
    )(u, wb, a_re, a_im, wc, d_skip.reshape(1, S5_WIDTH), w_glu, b_glu.reshape(1, S5_WIDTH))


def _s5_weights(lam_re, lam_im, b_re, b_im, c_re, c_im, log_step):
    step = jnp.exp(log_step.astype(F32))[:, None]
    lr, li = lam_re.astype(F32), lam_im.astype(F32)
    mag = jnp.exp(lr * step)
    ab_re, ab_im = mag * jnp.cos(li * step), mag * jnp.sin(li * step)
    den = lr * lr + li * li
    nr, ni = ab_re - 1.0, ab_im
    f_re, f_im = (nr * lr + ni * li) / den, (ni * lr - nr * li) / den
    br, bi = b_re.astype(F32), b_im.astype(F32)
    bb_re = f_re[..., None] * br - f_im[..., None] * bi
    bb_im = f_re[..., None] * bi + f_im[..., None] * br
    hg = S5_HALF_GROUPS
    eye = jnp.eye(hg, dtype=F32)

    def b_blockdiag(bb):
        bb = bb.reshape(2, hg, S5_STATE, S5_GROUP)
        return jnp.einsum('hgnc,gk->hgckn', bb, eye).reshape(2, hg * S5_GROUP, hg * S5_STATE)

    def c_blockdiag(cc):
        cc = cc.reshape(2, hg, S5_GROUP, S5_STATE)
        return jnp.einsum('hgcn,gk->hgnkc', cc, eye).reshape(2, hg * S5_STATE, hg * S5_GROUP)

    wb = jnp.concatenate([b_blockdiag(bb_re), b_blockdiag(bb_im)], axis=-1).astype(BF16)
    wc = jnp.concatenate([c_blockdiag(c_re.astype(F32)), -c_blockdiag(c_im.astype(F32))],
                         axis=1).astype(BF16)
    a_re = ab_re.reshape(2, 1, S5_HALF_STATE)
    a_im = ab_im.reshape(2, 1, S5_HALF_STATE)
    return wb, a_re, a_im, wc


def _conv_kernel(v_ref, wdw_ref, bdw_ref, lng_ref, lnb_ref, wpw_ref, o_ref, hbuf_ref, cbuf_ref):
    rows = v_ref.shape[0]
    halo = (CONV_K - 1) * SUBLANE

    @pl.when(pl.program_id(0) == 0)
    def _():
        hbuf_ref[0:halo, :] = jnp.zeros((halo, CONV_WIDTH), F32)

    @pl.when(pl.program_id(0) > 0)
    def _():
        hbuf_ref[0:halo, :] = hbuf_ref[rows:rows + halo, :]

    hbuf_ref[halo:halo + rows, :] = v_ref[:, 0:CONV_WIDTH] * _sigmoid(v_ref[:, CONV_WIDTH:])

    def chunk(c, carry):
        r0 = pl.multiple_of(c * CONV_ROWS, CONV_ROWS)
        acc = jnp.zeros((CONV_ROWS, CONV_WIDTH), F32)
        for k in range(CONV_K):
            acc = acc + wdw_ref[k:k + 1, :] * hbuf_ref[pl.ds(r0 + k * SUBLANE, CONV_ROWS), :]
        cbuf_ref[pl.ds(r0, CONV_ROWS), :] = acc
        return carry

    lax.fori_loop(0, rows // CONV_ROWS, chunk, 0)
    c = cbuf_ref[...] + bdw_ref[...]
    mu = jnp.mean(c, axis=-1, keepdims=True)
    var = jnp.mean(jnp.square(c - mu), axis=-1, keepdims=True)
    hn = _silu((c - mu) * lax.rsqrt(var + EPS) * lng_ref[...] + lnb_ref[...])
    y = _dot(hn.astype(BF16), wpw_ref[...])
    o_ref[...] = y.astype(o_ref.dtype)


def _conv(v, w_dw, b_dw, ln_g, ln_b, w_pw, bsz, s_len):
    rows = CONV_TB * bsz
    n_blk = s_len // CONV_TB
    halo = (CONV_K - 1) * SUBLANE
    const2 = lambda i: (0, 0)
    vec = lambda a: a.reshape(1, CONV_WIDTH)
    return pl.pallas_call(
        _conv_kernel,
        grid=(n_blk,),
        in_specs=[pl.BlockSpec((rows, 2 * CONV_WIDTH), lambda i: (i, 0)),
                  pl.BlockSpec((CONV_K, CONV_WIDTH), const2),
                  pl.BlockSpec((1, CONV_WIDTH), const2),
                  pl.BlockSpec((1, CONV_WIDTH), const2),
                  pl.BlockSpec((1, CONV_WIDTH), const2),
                  pl.BlockSpec(w_pw.shape, const2)],
        out_specs=pl.BlockSpec((rows, CONV_WIDTH), lambda i: (i, 0)),
        out_shape=jax.ShapeDtypeStruct((s_len * bsz, CONV_WIDTH), BF16),
        scratch_shapes=[pltpu.VMEM((halo + rows, CONV_WIDTH), F32),
                        pltpu.VMEM((rows, CONV_WIDTH), F32)],
        compiler_params=_cparams(("arbitrary",)),
        name="conv",
    )(v, w_dw, vec(b_dw), vec(ln_g), vec(ln_b), w_pw)


def _nsa_compress_kernel(kraw_ref, vraw_ref, pk_ref, pv_ref, kw1_ref, kw2_ref, vw1_ref, vw2_ref, ktab_ref,
                         ko_ref, vo_ref):
    n_chunks = ko_ref.shape[1]

    def compress(raw_ref, pos_ref, w1_ref, w2_ref, o_ref, tab):
        lo = jnp.zeros((n_chunks, NSA_KV_HEADS * CMP_HIDDEN), F32)
        hi = jnp.zeros((n_chunks, NSA_KV_HEADS * CMP_HIDDEN), F32)
        for l in range(CMP_STRIDE):
            x = raw_ref[pl.ds(l, n_chunks, stride=CMP_STRIDE), :]
            lo = lo + _dot((x + pos_ref[l]).astype(BF16), w1_ref[l])
            hi = hi + _dot((x + pos_ref[CMP_STRIDE + l]).astype(BF16), w1_ref[CMP_STRIDE + l])
        hid = _gelu_tanh(lo + pltpu.roll(hi, n_chunks - 1, 0))
        for y in range(NSA_KV_HEADS):
            out = _dot(hid[:, y * CMP_HIDDEN:(y + 1) * CMP_HIDDEN].astype(BF16), w2_ref[...])
            o_ref[y] = (out if tab is None else out + tab).astype(o_ref.dtype)

    compress(kraw_ref, pk_ref, kw1_ref, kw2_ref, ko_ref, ktab_ref[...])
    compress(vraw_ref, pv_ref, vw1_ref, vw2_ref, vo_ref, None)


def _nsa_compress(kraw, vraw, pos_k, pos_v, kw1, kw2, vw1, vw2, ktab, bsz, s_len):
    n_chunks = s_len // CMP_STRIDE
    const2 = lambda b: (0, 0)
    const3 = lambda b: (0, 0, 0)
    raw_spec = pl.BlockSpec((None, s_len, LANE), lambda b: (b, 0, 0))
    o_shape = jax.ShapeDtypeStruct((bsz, NSA_KV_HEADS, n_chunks, LANE), BF16)
    o_spec = pl.BlockSpec((None, NSA_KV_HEADS, n_chunks, LANE), lambda b: (b, 0, 0, 0))
    return pl.pallas_call(
        _nsa_compress_kernel,
        grid=(bsz,),
        in_specs=[raw_spec, raw_spec,
                  pl.BlockSpec(pos_k.shape, const3),
                  pl.BlockSpec(pos_v.shape, const3),
                  pl.BlockSpec(kw1.shape, const3),
                  pl.BlockSpec(kw2.shape, const2),
                  pl.BlockSpec(vw1.shape, const3),
                  pl.BlockSpec(vw2.shape, const2),
                  pl.BlockSpec(ktab.shape, const2)],
        out_specs=[o_spec, o_spec],
        out_shape=[o_shape, o_shape],
        compiler_params=_cparams(("parallel",)),
        name="nsa_compress",
    )(kraw, vraw, pos_k, pos_v, kw1, kw2, vw1, vw2, ktab)


def _nsa_attn_kernel(s_len, q_ref, kc_ref, vc_ref, ks_ref, vs_ref, kw_ref, vw_ref, gl_ref, gn_ref,
                     ov_ref, o_ref, m_ref, l_ref, acc_ref):
    tq, tk = NSA_TQ, NSA_TK
    y = pl.program_id(1)
    t0 = pl.program_id(2) * tq
    n_cmp = (s_len - CMP_LEN) // CMP_STRIDE + 1
    n_sel_blocks = s_len // SEL_LEN
    n_top = min(SEL_TOPN, n_sel_blocks)
    lane_i = lax.broadcasted_iota(jnp.int32, (tq, LANE), 1)
    tpos = t0 + lax.broadcasted_iota(jnp.int32, (tq, LANE), 0)
    slopes = [jnp.where(y == 0, 2.0 ** -(g + 1), 2.0 ** -(NSA_GQA + g + 1)).astype(F32)
              for g in range(NSA_GQA)]

    heads = range(NSA_GQA)

    def q_heads(extra):
        parts = []
        for g in heads:
            aug = jnp.where(lane_i == NSA_POS_HI, slopes[g] * 256.0,
                            jnp.where(lane_i == NSA_POS_LO, slopes[g], 0.0))
            if extra is not None:
                aug = aug + extra
            parts.append(q_ref[:, g * LANE:(g + 1) * LANE] + aug.astype(BF16))
        return parts

    mask_c = (tpos >= lane_i * CMP_STRIDE + (CMP_LEN - 1)) & (lane_i < n_cmp)
    kc, vc = kc_ref[...], vc_ref[...]
    p_c = [_masked_softmax(_dot_nt(q, kc), mask_c) for q in q_heads(None)]
    o_cmp = [_dot(p.astype(BF16), vc) for p in p_c]

    p_sum = (p_c[0] + p_c[1]) + (p_c[2] + p_c[3])
    p_hi = p_sum.astype(BF16)
    p_lo = (p_sum - p_hi.astype(F32)).astype(BF16)
    p_slc = _dot(p_hi, ov_ref[...]) + _dot(p_lo, ov_ref[...])
    cur = tpos // SEL_LEN
    forced = (lane_i == 0) | (lane_i == cur) | (lane_i == cur - 1)
    causal_blk = lane_i * SEL_LEN <= tpos
    score = jnp.where(forced, p_slc + 1e4, jnp.where(causal_blk, p_slc, -1e4))
    score_t = score.T[0:n_sel_blocks, :]
    j_idx = lax.broadcasted_iota(jnp.int32, score_t.shape, 0)
    rank = jnp.zeros(score_t.shape, F32)
    for blk in range(n_sel_blocks):
        row = score_t[blk:blk + 1, :]
        beats = (row > score_t) | ((row == score_t) & (j_idx > blk))
        rank = rank + jnp.where(beats, 1.0, 0.0)
    bias_t = jnp.where(rank < n_top, 0.0, MASKED)
    sel_bias = jnp.concatenate([bias_t, jnp.zeros((LANE - n_sel_blocks, tq), F32)], axis=0).T

    qs = q_heads(sel_bias)
    m_ref[...] = jnp.full_like(m_ref, M_INIT)
    l_ref[...] = jnp.zeros_like(l_ref)
    acc_ref[...] = jnp.zeros_like(acc_ref)

    def sel_tile(kt, causal_mask):
        k0 = pl.multiple_of(kt * tk, tk)
        k = ks_ref[pl.ds(k0, tk), :]
        v = vs_ref[pl.ds(k0, tk), :]
        scores = [_dot_nt(qs[g], k) for g in heads]
        if causal_mask:
            ok = (k0 + lax.broadcasted_iota(jnp.int32, (tq, tk), 1)
                  <= t0 + lax.broadcasted_iota(jnp.int32, (tq, tk), 0))
            scores = [jnp.where(ok, s, MASKED) for s in scores]
        _flash_tile(scores, v, m_ref, l_ref, acc_ref, [slice(g * tq, (g + 1) * tq) for g in heads])

    n_full = t0 // tk

    def full_tile(kt, carry):
        sel_tile(kt, False)
        return carry

    lax.fori_loop(0, n_full, full_tile, 0)
    sel_tile(n_full, True)

    span = min(WINDOW + tq, s_len)
    ws = pl.multiple_of(jnp.minimum(jnp.maximum(t0 - WINDOW, 0), s_len - span), tq)
    dist_w = ((t0 - ws) + lax.broadcasted_iota(jnp.int32, (tq, span), 0)
              - lax.broadcasted_iota(jnp.int32, (tq, span), 1))
    bias_w = jnp.where((dist_w >= 0) & (dist_w < WINDOW), 0.0, MASKED)
    kw = kw_ref[pl.ds(ws, span), :]
    vw = vw_ref[pl.ds(ws, span), :]
    s_w = [_dot_nt(qs[g], kw) + bias_w for g in heads]
    e_w = [jnp.exp(s - jnp.max(s, axis=-1, keepdims=True)) for s in s_w]
    sum_w = [jnp.maximum(jnp.sum(e, axis=-1, keepdims=True), 1e-30) for e in e_w]
    o_win = [_dot(e.astype(BF16), vw) / d for e, d in zip(e_w, sum_w)]

    gates = _sigmoid(gl_ref[...])
    outs = []
    for g in heads:
        r = slice(g * tq, (g + 1) * tq)
        o_sel = acc_ref[r, :] / jnp.maximum(l_ref[r, :], 1e-30)
        outs.append(gates[:, g:g + 1] * o_cmp[g]
                    + gates[:, NSA_GQA + g:NSA_GQA + g + 1] * o_sel
                    + gates[:, 2 * NSA_GQA + g:2 * NSA_GQA + g + 1] * o_win[g])
    lo = lane_i < NSA_HEAD_DIM
    o = jnp.concatenate([jnp.where(lo, outs[0], outs[1]), jnp.where(lo, outs[2], outs[3])], axis=-1)
    o_ref[...] = (o * _silu(gn_ref[...])).astype(o_ref.dtype)


def _nsa_attn(q, kc, vc, kvn, gl, gn, overlap, bsz, s_len):
    tq = NSA_TQ
    n_chunks = kc.shape[2]
    rows = NSA_GQA * tq
    kv_spec = lambda col: pl.BlockSpec((None, s_len, LANE), lambda b, y, i, col=col: (b, 0, col + y))
    c_spec = pl.BlockSpec((None, None, n_chunks, LANE), lambda b, y, i: (b, y, 0, 0))
    return pl.pallas_call(
        functools.partial(_nsa_attn_kernel, s_len),
        grid=(bsz, NSA_KV_HEADS, s_len // tq),
        in_specs=[pl.BlockSpec((None, tq, NSA_GQA * LANE), lambda b, y, i: (b, i, y)),
                  c_spec, c_spec,
                  kv_spec(0), kv_spec(2), kv_spec(4), kv_spec(6),
                  pl.BlockSpec((None, tq, LANE), lambda b, y, i: (b, i, y)),
                  pl.BlockSpec((None, tq, NSA_GQA * NSA_HEAD_DIM), lambda b, y, i: (b, i, y)),
                  pl.BlockSpec(overlap.shape, lambda b, y, i: (0, 0))],
        out_specs=pl.BlockSpec((None, tq, NSA_GQA * NSA_HEAD_DIM), lambda b, y, i: (b, i, y)),
        out_shape=jax.ShapeDtypeStruct((bsz, s_len, NSA_WIDTH), BF16),
        scratch_shapes=[pltpu.VMEM((rows, LANE), F32)] * 3,
        compiler_params=_cparams(("parallel", "parallel", "parallel")),
        name="nsa_attn",
    )(q, kc, vc, kvn, kvn, kvn, kvn, gl, gn, overlap)


def _rope_table_kernel(pos_ref, freq_ref, cos_ref, sin_ref):
    ang = pos_ref[...].astype(F32) * freq_ref[...]
    lane = lax.broadcasted_iota(jnp.int32, ang.shape, 1)
    half = MLA_ROPE // 2
    first = (lane >= MLA_NOPE) & (lane < MLA_NOPE + half)
    second = (lane >= MLA_NOPE + half) & (lane < MLA_NOPE + MLA_ROPE)
    cos_ref[...] = jnp.where(first | second, jnp.cos(ang), jnp.where(lane < MLA_NOPE, 1.0, 0.0))
    sn = jnp.sin(ang)
    sin_ref[...] = jnp.where(first, -sn, jnp.where(second, sn, 0.0))


def _rope_tables(positions, bsz, s_len):
    half = MLA_ROPE // 2
    freqs = ROPE_BASE ** (-jnp.arange(half, dtype=F32) / half)
    freq_lane = jnp.concatenate([jnp.zeros((MLA_NOPE,), F32), freqs, freqs,
                                 jnp.zeros((LANE - MLA_NOPE - MLA_ROPE,), F32)]).reshape(1, LANE)
    spec = pl.BlockSpec((None, TM, LANE), lambda b, i: (b, i, 0))
    shape = jax.ShapeDtypeStruct((bsz, s_len, LANE), F32)
    return pl.pallas_call(
        _rope_table_kernel,
        grid=(bsz, s_len // TM),
        in_specs=[pl.BlockSpec((None, TM, 1), lambda b, i: (b, i, 0)),
                  pl.BlockSpec((1, LANE), lambda b, i: (0, 0))],
        out_specs=[spec, spec],
        out_shape=[shape, shape],
        compiler_params=_cparams(("parallel", "parallel")),
        name="rope_tables",
    )(positions.reshape(bsz, s_len, 1), freq_lane)


def _mla_proj_kernel(cq_ref, ckr_ref, cos_ref, sin_ref, qn_ref, kvn_ref, wqa_ref, wqb_ref, wk_ref, wv_ref,
                     q_ref, k_ref, v_ref):
    cq = _rms(cq_ref[...], qn_ref[...]).astype(BF16)
    ckv = _rms(ckr_ref[:, 0:MLA_KV_RANK], kvn_ref[...]).astype(BF16)
    cosm = cos_ref[...]
    sinm = sin_ref[...]
    kr = ckr_ref[:, MLA_KV_RANK:MLA_KV_RANK + LANE] * cosm + ckr_ref[:, MLA_KV_RANK + LANE:] * sinm
    v_ref[...] = _dot(ckv, wv_ref[...]).astype(v_ref.dtype)
    for h in range(MLA_HEADS):
        c = slice(h * LANE, (h + 1) * LANE)
        q = _dot(cq, wqa_ref[:, c]) * cosm + _dot(cq, wqb_ref[:, c]) * sinm
        q_ref[:, c] = (q * MLA_SCALE).astype(q_ref.dtype)
        k_ref[:, c] = (_dot(ckv, wk_ref[:, c]) + kr).astype(k_ref.dtype)


def _mla_proj(cq, ckr, cosm, sinm, q_norm, kv_norm, wqa, wqb, wk, wv, bsz, s_len):
    const = lambda b, i: (0, 0)
    tok = lambda n: pl.BlockSpec((None, TM, n), lambda b, i: (b, i, 0))
    return pl.pallas_call(
        _mla_proj_kernel,
        grid=(bsz, s_len // TM),
        in_specs=[tok(MLA_Q_RANK), tok(MLA_KV_RANK + 2 * LANE), tok(LANE), tok(LANE),
                  pl.BlockSpec((1, MLA_Q_RANK), const), pl.BlockSpec((1, MLA_KV_RANK), const),
                  pl.BlockSpec(wqa.shape, const), pl.BlockSpec(wqb.shape, const),
                  pl.BlockSpec(wk.shape, const), pl.BlockSpec(wv.shape, const)],
        out_specs=[tok(MLA_HEADS * LANE), tok(MLA_HEADS * LANE), tok(MLA_WIDTH)],
        out_shape=[jax.ShapeDtypeStruct((bsz, s_len, MLA_HEADS * LANE), BF16),
                   jax.ShapeDtypeStruct((bsz, s_len, MLA_HEADS * LANE), BF16),
                   jax.ShapeDtypeStruct((bsz, s_len, MLA_WIDTH), BF16)],
        compiler_params=_cparams(("parallel", "parallel")),
        name="mla_proj",
    )(cq, ckr, cosm, sinm, q_norm.reshape(1, MLA_Q_RANK), kv_norm.reshape(1, MLA_KV_RANK), wqa, wqb, wk, wv)


def _mla_attn_kernel(q_ref, k_ref, v_ref, gm_ref, o_ref, m_ref, l_ref, acc_ref):
    tq, tk = MLA_TQ, MLA_TK
    t0 = pl.program_id(2) * tq
    n_full = t0 // tk
    sub = MLA_SUB_ROWS
    n_sub = tq // sub
    m_ref[...] = jnp.full_like(m_ref, M_INIT)
    l_ref[...] = jnp.zeros_like(l_ref)
    acc_ref[...] = jnp.zeros_like(acc_ref)

    def tile(kt, causal_mask):
        k0 = pl.multiple_of(kt * tk, tk)
        v = v_ref[pl.ds(k0, tk), :]
        scores, row_slices = [], []
        for hh in range(2):
            k = k_ref[pl.ds(k0, tk), hh * LANE:(hh + 1) * LANE]
            for rb in range(n_sub):
                s = _dot_nt(q_ref[rb * sub:(rb + 1) * sub, hh * LANE:(hh + 1) * LANE], k)
                if causal_mask:
                    ok = (k0 + lax.broadcasted_iota(jnp.int32, (sub, tk), 1)
                          <= t0 + rb * sub + lax.broadcasted_iota(jnp.int32, (sub, tk), 0))
                    s = jnp.where(ok, s, MASKED)
                scores.append(s)
                row_slices.append(slice(hh * tq + rb * sub, hh * tq + (rb + 1) * sub))
        _flash_tile(scores, v, m_ref, l_ref, acc_ref, row_slices)

    def full_tile(kt, carry):
        tile(kt, False)
        return carry

    lax.fori_loop(0, n_full, full_tile, 0)
    tile(n_full, True)
    o_all = acc_ref[...] / jnp.maximum(l_ref[...], 1e-30)
    lane = lax.broadcasted_iota(jnp.int32, (tq, LANE), 1)
    o = jnp.where(lane < MLA_V, o_all[0:tq], o_all[tq:2 * tq])
    o_ref[...] = (o * _silu(gm_ref[...])).astype(o_ref.dtype)


def _mla_attn(q, k, v, gm, bsz, s_len):
    tq = MLA_TQ
    return pl.pallas_call(
        _mla_attn_kernel,
        grid=(bsz, MLA_HEADS // 2, s_len // tq),
        in_specs=[pl.BlockSpec((None, tq, 2 * LANE), lambda b, hp, i: (b, i, hp)),
                  pl.BlockSpec((None, s_len, 2 * LANE), lambda b, hp, i: (b, 0, hp)),
                  pl.BlockSpec((None, s_len, LANE), lambda b, hp, i: (b, 0, hp)),
                  pl.BlockSpec((None, tq, LANE), lambda b, hp, i: (b, i, hp))],
        out_specs=pl.BlockSpec((None, tq, LANE), lambda b, hp, i: (b, i, hp)),
        out_shape=jax.ShapeDtypeStruct((bsz, s_len, MLA_WIDTH), BF16),
        scratch_shapes=[pltpu.VMEM((2 * tq, LANE), F32)] * 3,
        compiler_params=_cparams(("parallel", "parallel", "parallel")),
        name="mla_attn",
    )(q, k, v, gm)


def _odd_in_layout():
    off = np.cumsum([0, NSA_WIDTH, 6 * NSA_KV_HEADS * NSA_HEAD_DIM, 3 * NSA_HEADS, NSA_WIDTH,
                     MLA_Q_RANK, MLA_KV_RANK, MLA_ROPE, MLA_WIDTH])
    q0, kv0, gl0, gn0, cq0, ckv0, kr0, gm0 = off[:8]
    dh = NSA_HEAD_DIM
    zeros = lambda n: [-1] * n
    cols, segs, start = [], [], 0

    def seg(c):
        nonlocal start
        cols.extend(c)
        segs.append((start, start + len(c)))
        start += len(c)

    lead, tail = zeros(NSA_D0), zeros(LANE - NSA_D0 - dh)
    c = []
    for h in range(NSA_HEADS):
        c += lead + list(range(q0 + h * dh, q0 + (h + 1) * dh)) + tail
    seg(c)
    seg(list(range(kv0, kv0 + NSA_KV_HEADS * dh)))
    seg(list(range(kv0 + NSA_KV_HEADS * dh, kv0 + 2 * NSA_KV_HEADS * dh)))
    c = []
    for slot in range(2, 6):
        for y in range(NSA_KV_HEADS):
            base = kv0 + (slot * NSA_KV_HEADS + y) * dh
            src = list(range(base, base + dh))
            c += (src + src) if slot % 2 == 1 else (lead + src + tail)
    seg(c)
    c = []
    for y in range(NSA_KV_HEADS):
        blk = [gl0 + br * NSA_HEADS + y * NSA_GQA + g for br in range(3) for g in range(NSA_GQA)]
        c += blk + zeros(LANE - len(blk))
    seg(c)
    seg(list(range(gn0, gn0 + NSA_WIDTH)))
    seg(list(range(cq0, cq0 + MLA_Q_RANK)))
    half = MLA_ROPE // 2
    kr = list(range(kr0, kr0 + MLA_ROPE))
    pad_hi = zeros(LANE - MLA_NOPE - MLA_ROPE)
    seg(list(range(ckv0, ckv0 + MLA_KV_RANK))
        + zeros(MLA_NOPE) + kr + pad_hi
        + zeros(MLA_NOPE) + kr[half:] + kr[:half] + pad_hi)
    seg(list(range(gm0, gm0 + MLA_WIDTH)))
    return np.asarray(cols, np.int32), segs


_ODD_COLS, _ODD_SEGS = _odd_in_layout()
_ODD_DTYPES = (BF16, F32, F32, BF16, F32, F32, F32, F32, F32)
_ODD_KVN_SEG = 3
_ODD_COL_SCALE = np.where(np.arange(len(_ODD_COLS)) < _ODD_SEGS[0][1], NSA_HEAD_DIM ** -0.5, 1.0).astype(np.float32)


def _gather_cols(w, cols, col_scale=None):
    cols = np.asarray(cols)
    same_run = ((cols[1:] < 0) & (cols[:-1] < 0)) | ((cols[1:] >= 0) & (cols[:-1] >= 0) & (np.diff(cols) == 1))
    breaks = np.flatnonzero(~same_run) + 1
    parts = []
    for run in np.split(cols, breaks):
        if run[0] < 0:
            parts.append(jnp.zeros((w.shape[0], len(run)), w.dtype))
        else:
            parts.append(w[:, int(run[0]):int(run[0]) + len(run)])
    g = jnp.concatenate(parts, axis=1)
    return g if col_scale is None else g * jnp.asarray(col_scale)[None, :]


def _mla_weight_layout():
    dq = MLA_NOPE + MLA_ROPE
    half = MLA_ROPE // 2
    qa, qb, kn, vv = [], [], [], []
    for h in range(MLA_HEADS):
        nope = list(range(h * dq, h * dq + MLA_NOPE))
        rope = list(range(h * dq + MLA_NOPE, (h + 1) * dq))
        pad = [-1] * (LANE - dq)
        qa += nope + rope + pad
        qb += [-1] * MLA_NOPE + rope[half:] + rope[:half] + pad
        kv0 = h * (MLA_NOPE + MLA_V)
        kn += list(range(kv0, kv0 + MLA_NOPE)) + [-1] * (LANE - MLA_NOPE)
        vv += list(range(kv0 + MLA_NOPE, kv0 + MLA_NOPE + MLA_V))
    return tuple(np.asarray(c, np.int32) for c in (qa, qb, kn, vv))


_MLA_QA, _MLA_QB, _MLA_KN, _MLA_VV = _mla_weight_layout()


def _nsa_constants(s_len):
    n_cmp = (s_len - CMP_LEN) // CMP_STRIDE + 1
    n_sel = s_len // SEL_LEN
    cs = np.arange(LANE)[:, None] * CMP_STRIDE
    sb = np.arange(LANE)[None]
    overlap = ((cs < (sb + 1) * SEL_LEN) & (cs + CMP_LEN > sb * SEL_LEN)
               & (np.arange(LANE)[:, None] < n_cmp) & (sb < n_sel)).astype(np.float32)

    def pos_lanes(pos):
        tab = np.zeros((len(pos), LANE), np.float32)
        tab[:, NSA_POS_HI] = pos // 256
        tab[:, NSA_POS_LO] = pos % 256
        return tab

    t = np.arange(s_len)
    tab_win = pos_lanes(t)
    tab_sel = pos_lanes(t)
    tab_sel[t, t // SEL_LEN] = 1.0
    zero = np.zeros((s_len, LANE), np.float32)
    kvn_tab = np.concatenate([tab_sel, tab_sel, zero, zero, tab_win, tab_win, zero, zero], axis=1)
    cmp_tab = pos_lanes(np.arange(LANE) * CMP_STRIDE + (CMP_LEN - 1))
    return jnp.asarray(overlap, BF16), jnp.asarray(kvn_tab), jnp.asarray(cmp_tab)


_EVEN_SEGS = ((0, 512), (512, 1024), (1024, 2048), (2048, 2560))


def _even_layer(h, h_batch_major, p, layer, out_batch_major, bsz, s_len, pre_g, post_g, w_gate, w_ple,
                w_in, lam_re, lam_im, b_re, b_im, c_re, c_im, d_skip, log_step, w_glu, b_glu,
                w_dw, b_dw, ln_g, ln_b, w_pw, w_out):
    ua, ga, vb, gb = _in_proj(h, h_batch_major, pre_g, w_in.astype(BF16), _EVEN_SEGS, (F32,) * 4, True,
                              bsz, s_len)
    rows = s_len * bsz
    wb, a_re, a_im, wc = _s5_weights(lam_re, lam_im, b_re, b_im, c_re, c_im, log_step)
    ya = _s5(ua.reshape(rows, S5_WIDTH), wb, a_re, a_im, wc, d_skip, w_glu.astype(BF16), b_glu, bsz, s_len)
    yb = _conv(vb.reshape(rows, 2 * CONV_WIDTH), w_dw, b_dw, ln_g, ln_b, w_pw.astype(BF16), bsz, s_len)
    return _out_proj(ya.reshape(s_len, bsz * S5_WIDTH), yb.reshape(s_len, bsz * CONV_WIDTH), True,
                     h, h_batch_major, p, layer, w_out.astype(BF16), post_g, w_gate.astype(BF16),
                     w_ple.astype(BF16), out_batch_major, bsz, s_len, gates=(ga, gb))


def _odd_layer(h, p, layer, out_batch_major, bsz, s_len, rope_cos, rope_sin, pre_g, post_g, w_gate, w_ple,
               w_in, pos_k, pos_v, ck_w1, ck_w2, cv_w1, cv_w2, q_norm, kv_norm, w_uq, w_ukv, w_out):
    assert s_len // SEL_LEN <= NSA_SEL_LANES and s_len // CMP_STRIDE == LANE
    overlap, kvn_tab, cmp_tab = _nsa_constants(s_len)
    w_in_p = _gather_cols(w_in, _ODD_COLS, _ODD_COL_SCALE).astype(BF16)
    q, kraw, vraw, kvn, gl, gn, cq, ckr, gm = _in_proj(h, False, pre_g, w_in_p, _ODD_SEGS, _ODD_DTYPES, False,
                                                bsz, s_len, _ODD_KVN_SEG, kvn_tab)
    eye = jnp.eye(NSA_KV_HEADS, dtype=F32)

    def w1_blockdiag(w1):
        w = w1.reshape(CMP_LEN, NSA_HEAD_DIM, CMP_HIDDEN)
        return jnp.einsum('ldj,yz->lydzj', w, eye).reshape(
            CMP_LEN, NSA_KV_HEADS * NSA_HEAD_DIM, NSA_KV_HEADS * CMP_HIDDEN).astype(BF16)

    pos_lanes = lambda pos: jnp.tile(pos, (1, NSA_KV_HEADS)).reshape(CMP_LEN, 1, NSA_KV_HEADS * NSA_HEAD_DIM)
    kw2 = jnp.pad(ck_w2, ((0, 0), (NSA_D0, LANE - NSA_D0 - NSA_HEAD_DIM))).astype(BF16)
    vw2 = jnp.concatenate([cv_w2, cv_w2], axis=1).astype(BF16)
    kc, vc = _nsa_compress(kraw, vraw, pos_lanes(pos_k), pos_lanes(pos_v), w1_blockdiag(ck_w1), kw2,
                           w1_blockdiag(cv_w1), vw2, cmp_tab, bsz, s_len)
    y_c = _nsa_attn(q, kc, vc, kvn, gl, gn, overlap, bsz, s_len)
    qm, km, vm = _mla_proj(cq, ckr, rope_cos, rope_sin, q_norm, kv_norm,
                           _gather_cols(w_uq, _MLA_QA).astype(BF16), _gather_cols(w_uq, _MLA_QB).astype(BF16),
                           _gather_cols(w_ukv, _MLA_KN).astype(BF16), _gather_cols(w_ukv, _MLA_VV).astype(BF16),
                           bsz, s_len)
    y_d = _mla_attn(qm, km, vm, gm, bsz, s_len)
    return _out_proj(y_c, y_d, False, h, False, p, layer, w_out.astype(BF16), post_g, w_gate.astype(BF16),
                     w_ple.astype(BF16), out_batch_major, bsz, s_len)


def kernel(x, p, positions, pre_norm, post_norm, ple_gate, ple_proj, ev_w_in, s5_lam_re, s5_lam_im, s5_b_re, s5_b_im, s5_c_re, s5_c_im, s5_d, s5_log_step, s5_w_glu, s5_b_glu, cv_w_dw, cv_b_dw, cv_ln_g, cv_ln_b, cv_w_pw, ev_w_out, od_w_in, nsa_pos_k, nsa_pos_v, nsa_ck_w1, nsa_ck_w2, nsa_cv_w1, nsa_cv_w2, mla_q_norm, mla_kv_norm, mla_w_uq, mla_w_ukv, od_w_out):
    bsz, s_len, _ = x.shape
    assert bsz == SUBLANE and s_len % max(TM, NSA_TK, MLA_TK) == 0
    depth = p.shape[0]
    rope_cos, rope_sin = _rope_tables(positions, bsz, s_len)
    h = x
    for i in range(depth):
        j = i // 2
        last = i == depth - 1
        if i % 2 == 0:
            h = _even_layer(h, i == 0, p, i, last, bsz, s_len, pre_norm[i], post_norm[i], ple_gate[i],
                            ple_proj[i], ev_w_in[j], s5_lam_re[j], s5_lam_im[j], s5_b_re[j], s5_b_im[j],
                            s5_c_re[j], s5_c_im[j], s5_d[j], s5_log_step[j], s5_w_glu[j], s5_b_glu[j],
                            cv_w_dw[j], cv_b_dw[j], cv_ln_g[j], cv_ln_b[j], cv_w_pw[j], ev_w_out[j])
        else:
            h = _odd_layer(h, p, i, last, bsz, s_len, rope_cos, rope_sin, pre_norm[i], post_norm[i],
                           ple_gate[i], ple_proj[i], od_w_in[j], nsa_pos_k[j], nsa_pos_v[j], nsa_ck_w1[j],
                           nsa_ck_w2[j], nsa_cv_w1[j], nsa_cv_w2[j], mla_q_norm[j], mla_kv_norm[j],
                           mla_w_uq[j], mla_w_ukv[j], od_w_out[j])
    return h
```

```python
import functools
import math

import jax
import jax.numpy as jnp
import numpy as np
from jax import lax
from jax.experimental import pallas as pl
from jax.experimental.pallas import tpu as pltpu

F32 = jnp.float32
BF16 = jnp.bfloat16

D_MODEL = 1024
DEPTH = 4
PLE_DIM = 256
EPS = 1e-6
NEG = -1e30

S5_WIDTH = 512
S5_GROUP = 16
S5_GROUPS = 32
S5_STATE = 64
S5_HALF_GROUPS = 16
S5_HALF_STATE = S5_HALF_GROUPS * S5_STATE
CONV_WIDTH = 512
CONV_K = 31

NSA_HEADS = 8
NSA_KV_HEADS = 2
NSA_GQA = 4
NSA_HEAD_DIM = 64
NSA_WIDTH = 512
CMP_LEN = 32
CMP_STRIDE = 16
CMP_HIDDEN = 256
SEL_LEN = 64
SEL_TOPN = 8
WINDOW = 512

MLA_HEADS = 8
MLA_NOPE = 64
MLA_ROPE = 32
MLA_V = 64
MLA_Q_RANK = 256
MLA_KV_RANK = 128
MLA_WIDTH = 512
ROPE_BASE = 10000.0

LANE = 128
SUBLANE = 8
VMEM_LIMIT = 56 * 1024 * 1024

TM = 256
S5_TB = 64
CONV_TB = 64
CONV_ROWS = 32
NSA_TQ = 256
NSA_TK = 512
MLA_TQ = 512
MLA_TK = 512
MLA_SUB_ROWS = 256

NSA_SEL_LANES = 32
NSA_D0 = NSA_SEL_LANES
NSA_POS_HI = NSA_D0 + NSA_HEAD_DIM
NSA_POS_LO = NSA_POS_HI + 1
MASKED = -1e30
M_INIT = -1e29
MLA_SCALE = (MLA_NOPE + MLA_ROPE) ** -0.5


def _cparams(sem):
    return pltpu.CompilerParams(dimension_semantics=sem, vmem_limit_bytes=VMEM_LIMIT)


def _dot(a, b):
    return jnp.dot(a, b, preferred_element_type=F32)


def _dot_nt(a, b):
    return lax.dot_general(a, b, (((1,), (1,)), ((), ())), preferred_element_type=F32)


def _rms(x, g):
    return x * lax.rsqrt(jnp.mean(x * x, axis=-1, keepdims=True) + EPS) * g


def _sigmoid(x):
    return 1.0 / (1.0 + jnp.exp(-x))


def _silu(x):
    return x * _sigmoid(x)


def _gelu_tanh(x):
    c = math.sqrt(2.0 / math.pi)
    return 0.5 * x * (1.0 + jnp.tanh(c * (x + 0.044715 * (x * x * x))))


def _masked_softmax(s, mask):
    s = jnp.where(mask, s, NEG)
    m = jnp.max(s, axis=-1, keepdims=True)
    e = jnp.where(mask, jnp.exp(s - m), 0.0)
    return e / jnp.maximum(jnp.sum(e, axis=-1, keepdims=True), 1e-30)


def _flash_tile(scores, v, m_ref, l_ref, acc_ref, row_slices):
    n_rep = scores[0].shape[1] // LANE
    stats = []
    for s, r in zip(scores, row_slices):
        m_old = m_ref[r, :]
        m_new = jnp.maximum(m_old, jnp.max(s, axis=-1, keepdims=True))
        p = jnp.exp(s - jnp.concatenate([m_new] * n_rep, axis=1))
        stats.append((p.astype(BF16), jnp.exp(m_old - m_new), jnp.sum(p, axis=-1, keepdims=True), m_new))
    pvs = [_dot(p, v) for p, _, _, _ in stats]
    for (_, alpha, row_sum, m_new), pv, r in zip(stats, pvs, row_slices):
        l_ref[r, :] = alpha * l_ref[r, :] + row_sum
        acc_ref[r, :] = alpha * acc_ref[r, :] + pv
        m_ref[r, :] = m_new


def _in_proj_kernel(segs, tab_seg, x_ref, g_ref, w_ref, *refs):
    tab_ref, out_refs = (None, refs) if tab_seg is None else (refs[0], refs[1:])
    xn = _rms(x_ref[...], g_ref[...]).astype(BF16)
    for n, ((c0, c1), o_ref) in enumerate(zip(segs, out_refs)):
        y = _dot(xn, w_ref[:, c0:c1])
        if n == tab_seg:
            y = y + tab_ref[...]
        o_ref[...] = y.astype(o_ref.dtype)


def _in_proj(h, h_batch_major, gain, w, segs, out_dtypes, out_time_major, bsz, s_len, tab_seg=None, tab=None):
    n_t = s_len // TM
    if h_batch_major:
        x_spec = pl.BlockSpec((None, TM, D_MODEL), lambda b, i: (b, i, 0))
    else:
        x_spec = pl.BlockSpec((TM, D_MODEL), lambda b, i: (i, b))
    out_shapes, out_specs = [], []
    for (c0, c1), dt in zip(segs, out_dtypes):
        n = c1 - c0
        if out_time_major:
            out_shapes.append(jax.ShapeDtypeStruct((s_len, bsz * n), dt))
            out_specs.append(pl.BlockSpec((TM, n), lambda b, i: (i, b)))
        else:
            out_shapes.append(jax.ShapeDtypeStruct((bsz, s_len, n), dt))
            out_specs.append(pl.BlockSpec((None, TM, n), lambda b, i: (b, i, 0)))
    in_specs = [x_spec,
                pl.BlockSpec((1, D_MODEL), lambda b, i: (0, 0)),
                pl.BlockSpec(w.shape, lambda b, i: (0, 0))]
    args = [h, gain.reshape(1, D_MODEL), w]
    if tab_seg is not None:
        in_specs.append(pl.BlockSpec((TM, tab.shape[1]), lambda b, i: (i, 0)))
        args.append(tab)
    return pl.pallas_call(
        functools.partial(_in_proj_kernel, tuple(segs), tab_seg),
        grid=(bsz, n_t),
        in_specs=in_specs,
        out_specs=out_specs,
        out_shape=out_shapes,
        compiler_params=_cparams(("parallel", "parallel")),
        name="in_proj",
    )(*args)


def _out_proj_kernel(gated, ya_ref, yb_ref, *refs):
    if gated:
        ga_ref, gb_ref, h_ref, p_ref, wo_ref, gpost_ref, wg_ref, wp_ref, o_ref = refs
        ya = (ya_ref[...] * _silu(ga_ref[...])).astype(BF16)
        yb = (yb_ref[...] * _silu(gb_ref[...])).astype(BF16)
    else:
        h_ref, p_ref, wo_ref, gpost_ref, wg_ref, wp_ref, o_ref = refs
        ya, yb = ya_ref[...], yb_ref[...]
    half = wo_ref.shape[0] // 2
    y = _dot(ya, wo_ref[0:half, :]) + _dot(yb, wo_ref[half:, :])
    h1 = h_ref[...] + _rms(y, gpost_ref[...])
    gate = _sigmoid(_dot(h1.astype(BF16), wg_ref[...]))
    o_ref[...] = h1 + gate * _dot(p_ref[...].astype(BF16), wp_ref[...])


def _out_proj(ya, yb, y_time_major, h, h_batch_major, p, layer, w_out, g_post, w_gate, w_ple,
              out_batch_major, bsz, s_len, gates=None):
    n_t = s_len // TM
    half = w_out.shape[0] // 2
    if y_time_major:
        y_spec = pl.BlockSpec((TM, half), lambda b, i: (i, b))
    else:
        y_spec = pl.BlockSpec((None, TM, half), lambda b, i: (b, i, 0))
    if h_batch_major:
        h_spec = pl.BlockSpec((None, TM, D_MODEL), lambda b, i: (b, i, 0))
    else:
        h_spec = pl.BlockSpec((TM, D_MODEL), lambda b, i: (i, b))
    if out_batch_major:
        o_shape = jax.ShapeDtypeStruct((bsz, s_len, D_MODEL), F32)
        o_spec = pl.BlockSpec((None, TM, D_MODEL), lambda b, i: (b, i, 0))
    else:
        o_shape = jax.ShapeDtypeStruct((s_len, bsz * D_MODEL), F32)
        o_spec = pl.BlockSpec((TM, D_MODEL), lambda b, i: (i, b))
    const = lambda b, i: (0, 0)
    gated = gates is not None
    return pl.pallas_call(
        functools.partial(_out_proj_kernel, gated),
        grid=(bsz, n_t),
        in_specs=[y_spec, y_spec] + ([y_spec, y_spec] if gated else []) + [
                  h_spec,
                  pl.BlockSpec((None, None, TM, PLE_DIM), lambda b, i: (layer, b, i, 0)),
                  pl.BlockSpec(w_out.shape, const),
                  pl.BlockSpec((1, D_MODEL), const),
                  pl.BlockSpec(w_gate.shape, const),
                  pl.BlockSpec(w_ple.shape, const)],
        out_specs=o_spec,
        out_shape=o_shape,
        compiler_params=_cparams(("parallel", "parallel")),
        name="out_proj",
    )(ya, yb, *(gates or ()), h, p, w_out, g_post.reshape(1, D_MODEL), w_gate, w_ple)


def _rows_from_batch_cols(src_ref, slab_ref, width):
    steps = src_ref.shape[0]
    for b in range(SUBLANE):
        for j in range(width // LANE):
            c0 = b * width + j * LANE
            slab_ref[j, pl.ds(b, steps, stride=SUBLANE), :] = src_ref[:, c0:c0 + LANE]


def _batch_cols_from_rows(slab_ref, dst_ref, width):
    steps = dst_ref.shape[0]
    for b in range(SUBLANE):
        for j in range(width // LANE):
            c0 = b * width + j * LANE
            dst_ref[:, c0:c0 + LANE] = slab_ref[j, pl.ds(b, steps, stride=SUBLANE), :].astype(dst_ref.dtype)


def _s5_kernel(u_ref, wb_ref, are_ref, aim_ref, wc_ref, d_ref, wglu_ref, bglu_ref,
               o_ref, buf_ref, st_ref, slab_ref):
    n_steps = u_ref.shape[0]
    rows = n_steps * SUBLANE
    hs = S5_HALF_STATE
    n_slabs = S5_WIDTH // LANE

    @pl.when(pl.program_id(0) == 0)
    def _():
        st_ref[...] = jnp.zeros_like(st_ref)

    _rows_from_batch_cols(u_ref, slab_ref, S5_WIDTH)
    u = jnp.concatenate([slab_ref[j] for j in range(n_slabs)], axis=-1)
    ub = u.astype(BF16)
    ys = []
    for hf in range(2):
        c0 = hf * 2 * hs
        buf_ref[:, c0:c0 + 2 * hs] = _dot(ub[:, hf * 256:(hf + 1) * 256], wb_ref[hf])
        a_re = jnp.broadcast_to(are_ref[hf], (SUBLANE, hs))
        a_im = jnp.broadcast_to(aim_ref[hf], (SUBLANE, hs))

        def step(t, carry, c0=c0, a_re=a_re, a_im=a_im):
            x_re, x_im = carry
            r0 = pl.multiple_of(t * SUBLANE, SUBLANE)
            b_re = buf_ref[pl.ds(r0, SUBLANE), c0:c0 + hs]
            b_im = buf_ref[pl.ds(r0, SUBLANE), c0 + hs:c0 + 2 * hs]
            n_re = a_re * x_re - a_im * x_im + b_re
            n_im = a_re * x_im + a_im * x_re + b_im
            buf_ref[pl.ds(r0, SUBLANE), c0:c0 + hs] = n_re
            buf_ref[pl.ds(r0, SUBLANE), c0 + hs:c0 + 2 * hs] = n_im
            return n_re, n_im

        x_re, x_im = lax.fori_loop(
            0, n_steps, step, (st_ref[:, c0:c0 + hs], st_ref[:, c0 + hs:c0 + 2 * hs]))
        st_ref[:, c0:c0 + hs] = x_re
        st_ref[:, c0 + hs:c0 + 2 * hs] = x_im
        ys.append(_dot(buf_ref[:, c0:c0 + 2 * hs].astype(BF16), wc_ref[hf]))
    y = jnp.concatenate(ys, axis=-1) + d_ref[...] * u
    y = _gelu_tanh(y)
    y = y * _sigmoid(_dot(y.astype(BF16), wglu_ref[...]) + bglu_ref[...])
    for j in range(n_slabs):
        slab_ref[j] = y[:, j * LANE:(j + 1) * LANE]
    _batch_cols_from_rows(slab_ref, o_ref, S5_WIDTH)


def _s5(u, wb, a_re, a_im, wc, d_skip, w_glu, b_glu, bsz, s_len):
    rows = S5_TB * bsz
    n_blk = s_len // S5_TB
    const2 = lambda i: (0, 0)
    const3 = lambda i: (0, 0, 0)
    return pl.pallas_call(
        _s5_kernel,
        grid=(n_blk,),
        in_specs=[pl.BlockSpec((S5_TB, bsz * S5_WIDTH), lambda i: (i, 0)),
                  pl.BlockSpec(wb.shape, const3),
                  pl.BlockSpec(a_re.shape, const3),
                  pl.BlockSpec(a_im.shape, const3),
                  pl.BlockSpec(wc.shape, const3),
                  pl.BlockSpec((1, S5_WIDTH), const2),
                  pl.BlockSpec(w_glu.shape, const2),
                  pl.BlockSpec((1, S5_WIDTH), const2)],
        out_specs=pl.BlockSpec((S5_TB, bsz * S5_WIDTH), lambda i: (i, 0)),
        out_shape=jax.ShapeDtypeStruct((s_len, bsz * S5_WIDTH), BF16),
        scratch_shapes=[pltpu.VMEM((rows, 4 * S5_HALF_STATE), F32),
                        pltpu.VMEM((SUBLANE, 4 * S5_HALF_STATE), F32),
                        pltpu.VMEM((S5_WIDTH // LANE, rows, LANE), F32)],
        compiler_params=_cparams(("arbitrary",)),
        name="s5",
    )(u, wb, a_re, a_im, wc, d_skip.reshape(1, S5_WIDTH), w_glu, b_glu.reshape(1, S5_WIDTH))


def _s5_weights(lam_re, lam_im, b_re, b_im, c_re, c_im, log_step):
    step = jnp.exp(log_step.astype(F32))[:, None]
    lr, li = lam_re.astype(F32), lam_im.astype(F32)
    mag = jnp.exp(lr * step)
    ab_re, ab_im = mag * jnp.cos(li * step), mag * jnp.sin(li * step)
    den = lr * lr + li * li
    nr, ni = ab_re - 1.0, ab_im
    f_re, f_im = (nr * lr + ni * li) / den, (ni * lr - nr * li) / den
    br, bi = b_re.astype(F32), b_im.astype(F32)
    bb_re = f_re[..., None] * br - f_im[..., None] * bi
    bb_im = f_re[..., None] * bi + f_im[..., None] * br
    hg = S5_HALF_GROUPS
    eye = jnp.eye(hg, dtype=F32)

    def b_blockdiag(bb):
        bb = bb.reshape(2, hg, S5_STATE, S5_GROUP)
        return jnp.einsum('hgnc,gk->hgckn', bb, eye).reshape(2, hg * S5_GROUP, hg * S5_STATE)

    def c_blockdiag(cc):
        cc = cc.reshape(2, hg, S5_GROUP, S5_STATE)
        return jnp.einsum('hgcn,gk->hgnkc', cc, eye).reshape(2, hg * S5_STATE, hg * S5_GROUP)

    wb = jnp.concatenate([b_blockdiag(bb_re), b_blockdiag(bb_im)], axis=-1).astype(BF16)
    wc = jnp.concatenate([c_blockdiag(c_re.astype(F32)), -c_blockdiag(c_im.astype(F32))],
                         axis=1).astype(BF16)
    a_re = ab_re.reshape(2, 1, S5_HALF_STATE)
    a_im = ab_im.reshape(2, 1, S5_HALF_STATE)
    return wb, a_re, a_im, wc


def _conv_kernel(v_ref, wdw_ref, bdw_ref, lng_ref, lnb_ref, wpw_ref, o_ref, hbuf_ref, cbuf_ref, slab_ref):
    rows = v_ref.shape[0] * SUBLANE
    halo = (CONV_K - 1) * SUBLANE
    n_slabs = CONV_WIDTH // LANE

    @pl.when(pl.program_id(0) == 0)
    def _():
        hbuf_ref[0:halo, :] = jnp.zeros((halo, CONV_WIDTH), F32)

    @pl.when(pl.program_id(0) > 0)
    def _():
        hbuf_ref[0:halo, :] = hbuf_ref[rows:rows + halo, :]

    _rows_from_batch_cols(v_ref, slab_ref, 2 * CONV_WIDTH)
    for j in range(n_slabs):
        hbuf_ref[halo:halo + rows, j * LANE:(j + 1) * LANE] = slab_ref[j] * _sigmoid(slab_ref[n_slabs + j])

    def chunk(c, carry):
        r0 = pl.multiple_of(c * CONV_ROWS, CONV_ROWS)
        acc = jnp.zeros((CONV_ROWS, CONV_WIDTH), F32)
        for k in range(CONV_K):
            acc = acc + wdw_ref[k:k + 1, :] * hbuf_ref[pl.ds(r0 + k * SUBLANE, CONV_ROWS), :]
        cbuf_ref[pl.ds(r0, CONV_ROWS), :] = acc
        return carry

    lax.fori_loop(0, rows // CONV_ROWS, chunk, 0)
    c = cbuf_ref[...] + bdw_ref[...]
    mu = jnp.mean(c, axis=-1, keepdims=True)
    var = jnp.mean(jnp.square(c - mu), axis=-1, keepdims=True)
    hn = _silu((c - mu) * lax.rsqrt(var + EPS) * lng_ref[...] + lnb_ref[...])
    y = _dot(hn.astype(BF16), wpw_ref[...])
    for j in range(n_slabs):
        slab_ref[j] = y[:, j * LANE:(j + 1) * LANE]
    _batch_cols_from_rows(slab_ref, o_ref, CONV_WIDTH)


def _conv(v, w_dw, b_dw, ln_g, ln_b, w_pw, bsz, s_len):
    rows = CONV_TB * bsz
    n_blk = s_len // CONV_TB
    halo = (CONV_K - 1) * SUBLANE
    const2 = lambda i: (0, 0)
    vec = lambda a: a.reshape(1, CONV_WIDTH)
    return pl.pallas_call(
        _conv_kernel,
        grid=(n_blk,),
        in_specs=[pl.BlockSpec((CONV_TB, bsz * 2 * CONV_WIDTH), lambda i: (i, 0)),
                  pl.BlockSpec((CONV_K, CONV_WIDTH), const2),
                  pl.BlockSpec((1, CONV_WIDTH), const2),
                  pl.BlockSpec((1, CONV_WIDTH), const2),
                  pl.BlockSpec((1, CONV_WIDTH), const2),
                  pl.BlockSpec(w_pw.shape, const2)],
        out_specs=pl.BlockSpec((CONV_TB, bsz * CONV_WIDTH), lambda i: (i, 0)),
        out_shape=jax.ShapeDtypeStruct((s_len, bsz * CONV_WIDTH), BF16),
        scratch_shapes=[pltpu.VMEM((halo + rows, CONV_WIDTH), F32),
                        pltpu.VMEM((rows, CONV_WIDTH), F32),
                        pltpu.VMEM((2 * CONV_WIDTH // LANE, rows, LANE), F32)],
        compiler_params=_cparams(("arbitrary",)),
        name="conv",
    )(v, w_dw, vec(b_dw), vec(ln_g), vec(ln_b), w_pw)


def _nsa_compress_kernel(kraw_ref, vraw_ref, pk_ref, pv_ref, kw1_ref, kw2_ref, vw1_ref, vw2_ref, ktab_ref,
                         ko_ref, vo_ref):
    n_chunks = ko_ref.shape[1]

    def compress(raw_ref, pos_ref, w1_ref, w2_ref, o_ref, tab):
        lo = jnp.zeros((n_chunks, NSA_KV_HEADS * CMP_HIDDEN), F32)
        hi = jnp.zeros((n_chunks, NSA_KV_HEADS * CMP_HIDDEN), F32)
        for l in range(CMP_STRIDE):
            x = raw_ref[pl.ds(l, n_chunks, stride=CMP_STRIDE), :]
            lo = lo + _dot((x + pos_ref[l]).astype(BF16), w1_ref[l])
            hi = hi + _dot((x + pos_ref[CMP_STRIDE + l]).astype(BF16), w1_ref[CMP_STRIDE + l])
        hid = _gelu_tanh(lo + pltpu.roll(hi, n_chunks - 1, 0))
        for y in range(NSA_KV_HEADS):
            out = _dot(hid[:, y * CMP_HIDDEN:(y + 1) * CMP_HIDDEN].astype(BF16), w2_ref[...])
            o_ref[y] = (out if tab is None else out + tab).astype(o_ref.dtype)

    compress(kraw_ref, pk_ref, kw1_ref, kw2_ref, ko_ref, ktab_ref[...])
    compress(vraw_ref, pv_ref, vw1_ref, vw2_ref, vo_ref, None)


def _nsa_compress(kraw, vraw, pos_k, pos_v, kw1, kw2, vw1, vw2, ktab, bsz, s_len):
    n_chunks = s_len // CMP_STRIDE
    const2 = lambda b: (0, 0)
    const3 = lambda b: (0, 0, 0)
    raw_spec = pl.BlockSpec((None, s_len, LANE), lambda b: (b, 0, 0))
    o_shape = jax.ShapeDtypeStruct((bsz, NSA_KV_HEADS, n_chunks, LANE), BF16)
    o_spec = pl.BlockSpec((None, NSA_KV_HEADS, n_chunks, LANE), lambda b: (b, 0, 0, 0))
    return pl.pallas_call(
        _nsa_compress_kernel,
        grid=(bsz,),
        in_specs=[raw_spec, raw_spec,
                  pl.BlockSpec(pos_k.shape, const3),
                  pl.BlockSpec(pos_v.shape, const3),
                  pl.BlockSpec(kw1.shape, const3),
                  pl.BlockSpec(kw2.shape, const2),
                  pl.BlockSpec(vw1.shape, const3),
                  pl.BlockSpec(vw2.shape, const2),
                  pl.BlockSpec(ktab.shape, const2)],
        out_specs=[o_spec, o_spec],
        out_shape=[o_shape, o_shape],
        compiler_params=_cparams(("parallel",)),
        name="nsa_compress",
    )(kraw, vraw, pos_k, pos_v, kw1, kw2, vw1, vw2, ktab)


def _nsa_attn_kernel(s_len, q_ref, kc_ref, vc_ref, ks_ref, vs_ref, kw_ref, vw_ref, gl_ref, gn_ref,
                     ov_ref, o_ref, m_ref, l_ref, acc_ref):
    tq, tk = NSA_TQ, NSA_TK
    y = pl.program_id(1)
    t0 = pl.program_id(2) * tq
    n_cmp = (s_len - CMP_LEN) // CMP_STRIDE + 1
    n_sel_blocks = s_len // SEL_LEN
    n_top = min(SEL_TOPN, n_sel_blocks)
    lane_i = lax.broadcasted_iota(jnp.int32, (tq, LANE), 1)
    tpos = t0 + lax.broadcasted_iota(jnp.int32, (tq, LANE), 0)
    slopes = [jnp.where(y == 0, 2.0 ** -(g + 1), 2.0 ** -(NSA_GQA + g + 1)).astype(F32)
              for g in range(NSA_GQA)]

    heads = range(NSA_GQA)

    def q_heads(extra):
        parts = []
        for g in heads:
            aug = jnp.where(lane_i == NSA_POS_HI, slopes[g] * 256.0,
                            jnp.where(lane_i == NSA_POS_LO, slopes[g], 0.0))
            if extra is not None:
                aug = aug + extra
            parts.append(q_ref[:, g * LANE:(g + 1) * LANE] + aug.astype(BF16))
        return parts

    mask_c = (tpos >= lane_i * CMP_STRIDE + (CMP_LEN - 1)) & (lane_i < n_cmp)
    kc, vc = kc_ref[...], vc_ref[...]
    p_c = [_masked_softmax(_dot_nt(q, kc), mask_c) for q in q_heads(None)]
    o_cmp = [_dot(p.astype(BF16), vc) for p in p_c]

    p_sum = (p_c[0] + p_c[1]) + (p_c[2] + p_c[3])
    p_hi = p_sum.astype(BF16)
    p_lo = (p_sum - p_hi.astype(F32)).astype(BF16)
    p_slc = _dot(p_hi, ov_ref[...]) + _dot(p_lo, ov_ref[...])
    cur = tpos // SEL_LEN
    forced = (lane_i == 0) | (lane_i == cur) | (lane_i == cur - 1)
    causal_blk = lane_i * SEL_LEN <= tpos
    score = jnp.where(forced, p_slc + 1e4, jnp.where(causal_blk, p_slc, -1e4))
    score_t = score.T[0:n_sel_blocks, :]
    j_idx = lax.broadcasted_iota(jnp.int32, score_t.shape, 0)
    rank = jnp.zeros(score_t.shape, F32)
    for blk in range(n_sel_blocks):
        row = score_t[blk:blk + 1, :]
        beats = (row > score_t) | ((row == score_t) & (j_idx > blk))
        rank = rank + jnp.where(beats, 1.0, 0.0)
    bias_t = jnp.where(rank < n_top, 0.0, MASKED)
    sel_bias = jnp.concatenate([bias_t, jnp.zeros((LANE - n_sel_blocks, tq), F32)], axis=0).T

    qs = q_heads(sel_bias)
    m_ref[...] = jnp.full_like(m_ref, M_INIT)
    l_ref[...] = jnp.zeros_like(l_ref)
    acc_ref[...] = jnp.zeros_like(acc_ref)

    def sel_tile(kt, causal_mask):
        k0 = pl.multiple_of(kt * tk, tk)
        k = ks_ref[pl.ds(k0, tk), :]
        v = vs_ref[pl.ds(k0, tk), :]
        scores = [_dot_nt(qs[g], k) for g in heads]
        if causal_mask:
            ok = (k0 + lax.broadcasted_iota(jnp.int32, (tq, tk), 1)
                  <= t0 + lax.broadcasted_iota(jnp.int32, (tq, tk), 0))
            scores = [jnp.where(ok, s, MASKED) for s in scores]
        _flash_tile(scores, v, m_ref, l_ref, acc_ref, [slice(g * tq, (g + 1) * tq) for g in heads])

    n_full = t0 // tk

    def full_tile(kt, carry):
        sel_tile(kt, False)
        return carry

    lax.fori_loop(0, n_full, full_tile, 0)
    sel_tile(n_full, True)

    span = min(WINDOW + tq, s_len)
    ws = pl.multiple_of(jnp.minimum(jnp.maximum(t0 - WINDOW, 0), s_len - span), tq)
    dist_w = ((t0 - ws) + lax.broadcasted_iota(jnp.int32, (tq, span), 0)
              - lax.broadcasted_iota(jnp.int32, (tq, span), 1))
    bias_w = jnp.where((dist_w >= 0) & (dist_w < WINDOW), 0.0, MASKED)
    kw = kw_ref[pl.ds(ws, span), :]
    vw = vw_ref[pl.ds(ws, span), :]
    s_w = [_dot_nt(qs[g], kw) + bias_w for g in heads]
    e_w = [jnp.exp(s - jnp.max(s, axis=-1, keepdims=True)) for s in s_w]
    sum_w = [jnp.maximum(jnp.sum(e, axis=-1, keepdims=True), 1e-30) for e in e_w]
    o_win = [_dot(e.astype(BF16), vw) / d for e, d in zip(e_w, sum_w)]

    gates = _sigmoid(gl_ref[...])
    outs = []
    for g in heads:
        r = slice(g * tq, (g + 1) * tq)
        o_sel = acc_ref[r, :] / jnp.maximum(l_ref[r, :], 1e-30)
        outs.append(gates[:, g:g + 1] * o_cmp[g]
                    + gates[:, NSA_GQA + g:NSA_GQA + g + 1] * o_sel
                    + gates[:, 2 * NSA_GQA + g:2 * NSA_GQA + g + 1] * o_win[g])
    lo = lane_i < NSA_HEAD_DIM
    o = jnp.concatenate([jnp.where(lo, outs[0], outs[1]), jnp.where(lo, outs[2], outs[3])], axis=-1)
    o_ref[...] = (o * _silu(gn_ref[...])).astype(o_ref.dtype)


def _nsa_attn(q, kc, vc, kvn, gl, gn, overlap, bsz, s_len):
    tq = NSA_TQ
    n_chunks = kc.shape[2]
    rows = NSA_GQA * tq
    kv_spec = lambda col: pl.BlockSpec((None, s_len, LANE), lambda b, y, i, col=col: (b, 0, col + y))
    c_spec = pl.BlockSpec((None, None, n_chunks, LANE), lambda b, y, i: (b, y, 0, 0))
    return pl.pallas_call(
        functools.partial(_nsa_attn_kernel, s_len),
        grid=(bsz, NSA_KV_HEADS, s_len // tq),
        in_specs=[pl.BlockSpec((None, tq, NSA_GQA * LANE), lambda b, y, i: (b, i, y)),
                  c_spec, c_spec,
                  kv_spec(0), kv_spec(2), kv_spec(4), kv_spec(6),
                  pl.BlockSpec((None, tq, LANE), lambda b, y, i: (b, i, y)),
                  pl.BlockSpec((None, tq, NSA_GQA * NSA_HEAD_DIM), lambda b, y, i: (b, i, y)),
                  pl.BlockSpec(overlap.shape, lambda b, y, i: (0, 0))],
        out_specs=pl.BlockSpec((None, tq, NSA_GQA * NSA_HEAD_DIM), lambda b, y, i: (b, i, y)),
        out_shape=jax.ShapeDtypeStruct((bsz, s_len, NSA_WIDTH), BF16),
        scratch_shapes=[pltpu.VMEM((rows, LANE), F32)] * 3,
        compiler_params=_cparams(("parallel", "parallel", "parallel")),
        name="nsa_attn",
    )(q, kc, vc, kvn, kvn, kvn, kvn, gl, gn, overlap)


def _rope_table_kernel(pos_ref, freq_ref, cos_ref, sin_ref):
    ang = pos_ref[...].astype(F32) * freq_ref[...]
    lane = lax.broadcasted_iota(jnp.int32, ang.shape, 1)
    half = MLA_ROPE // 2
    first = (lane >= MLA_NOPE) & (lane < MLA_NOPE + half)
    second = (lane >= MLA_NOPE + half) & (lane < MLA_NOPE + MLA_ROPE)
    cos_ref[...] = jnp.where(first | second, jnp.cos(ang), jnp.where(lane < MLA_NOPE, 1.0, 0.0))
    sn = jnp.sin(ang)
    sin_ref[...] = jnp.where(first, -sn, jnp.where(second, sn, 0.0))


def _rope_tables(positions, bsz, s_len):
    half = MLA_ROPE // 2
    freqs = ROPE_BASE ** (-jnp.arange(half, dtype=F32) / half)
    freq_lane = jnp.concatenate([jnp.zeros((MLA_NOPE,), F32), freqs, freqs,
                                 jnp.zeros((LANE - MLA_NOPE - MLA_ROPE,), F32)]).reshape(1, LANE)
    spec = pl.BlockSpec((None, TM, LANE), lambda b, i: (b, i, 0))
    shape = jax.ShapeDtypeStruct((bsz, s_len, LANE), F32)
    return pl.pallas_call(
        _rope_table_kernel,
        grid=(bsz, s_len // TM),
        in_specs=[pl.BlockSpec((None, TM, 1), lambda b, i: (b, i, 0)),
                  pl.BlockSpec((1, LANE), lambda b, i: (0, 0))],
        out_specs=[spec, spec],
        out_shape=[shape, shape],
        compiler_params=_cparams(("parallel", "parallel")),
        name="rope_tables",
    )(positions.reshape(bsz, s_len, 1), freq_lane)


def _mla_proj_kernel(cq_ref, ckr_ref, cos_ref, sin_ref, qn_ref, kvn_ref, wqa_ref, wqb_ref, wk_ref, wv_ref,
                     q_ref, k_ref, v_ref):
    cq = _rms(cq_ref[...], qn_ref[...]).astype(BF16)
    ckv = _rms(ckr_ref[:, 0:MLA_KV_RANK], kvn_ref[...]).astype(BF16)
    cosm = cos_ref[...]
    sinm = sin_ref[...]
    kr = ckr_ref[:, MLA_KV_RANK:MLA_KV_RANK + LANE] * cosm + ckr_ref[:, MLA_KV_RANK + LANE:] * sinm
    v_ref[...] = _dot(ckv, wv_ref[...]).astype(v_ref.dtype)
    for h in range(MLA_HEADS):
        c = slice(h * LANE, (h + 1) * LANE)
        q = _dot(cq, wqa_ref[:, c]) * cosm + _dot(cq, wqb_ref[:, c]) * sinm
        q_ref[:, c] = (q * MLA_SCALE).astype(q_ref.dtype)
        k_ref[:, c] = (_dot(ckv, wk_ref[:, c]) + kr).astype(k_ref.dtype)


def _mla_proj(cq, ckr, cosm, sinm, q_norm, kv_norm, wqa, wqb, wk, wv, bsz, s_len):
    const = lambda b, i: (0, 0)
    tok = lambda n: pl.BlockSpec((None, TM, n), lambda b, i: (b, i, 0))
    return pl.pallas_call(
        _mla_proj_kernel,
        grid=(bsz, s_len // TM),
        in_specs=[tok(MLA_Q_RANK), tok(MLA_KV_RANK + 2 * LANE), tok(LANE), tok(LANE),
                  pl.BlockSpec((1, MLA_Q_RANK), const), pl.BlockSpec((1, MLA_KV_RANK), const),
                  pl.BlockSpec(wqa.shape, const), pl.BlockSpec(wqb.shape, const),
                  pl.BlockSpec(wk.shape, const), pl.BlockSpec(wv.shape, const)],
        out_specs=[tok(MLA_HEADS * LANE), tok(MLA_HEADS * LANE), tok(MLA_WIDTH)],
        out_shape=[jax.ShapeDtypeStruct((bsz, s_len, MLA_HEADS * LANE), BF16),
                   jax.ShapeDtypeStruct((bsz, s_len, MLA_HEADS * LANE), BF16),
                   jax.ShapeDtypeStruct((bsz, s_len, MLA_WIDTH), BF16)],
        compiler_params=_cparams(("parallel", "parallel")),
        name="mla_proj",
    )(cq, ckr, cosm, sinm, q_norm.reshape(1, MLA_Q_RANK), kv_norm.reshape(1, MLA_KV_RANK), wqa, wqb, wk, wv)


def _mla_attn_kernel(q_ref, k_ref, v_ref, gm_ref, o_ref, m_ref, l_ref, acc_ref):
    tq, tk = MLA_TQ, MLA_TK
    t0 = pl.program_id(2) * tq
    n_full = t0 // tk
    sub = MLA_SUB_ROWS
    n_sub = tq // sub
    m_ref[...] = jnp.full_like(m_ref, M_INIT)
    l_ref[...] = jnp.zeros_like(l_ref)
    acc_ref[...] = jnp.zeros_like(acc_ref)

    def tile(kt, causal_mask):
        k0 = pl.multiple_of(kt * tk, tk)
        v = v_ref[pl.ds(k0, tk), :]
        scores, row_slices = [], []
        for hh in range(2):
            k = k_ref[pl.ds(k0, tk), hh * LANE:(hh + 1) * LANE]
            for rb in range(n_sub):
                s = _dot_nt(q_ref[rb * sub:(rb + 1) * sub, hh * LANE:(hh + 1) * LANE], k)
                if causal_mask:
                    ok = (k0 + lax.broadcasted_iota(jnp.int32, (sub, tk), 1)
                          <= t0 + rb * sub + lax.broadcasted_iota(jnp.int32, (sub, tk), 0))
                    s = jnp.where(ok, s, MASKED)
                scores.append(s)
                row_slices.append(slice(hh * tq + rb * sub, hh * tq + (rb + 1) * sub))
        _flash_tile(scores, v, m_ref, l_ref, acc_ref, row_slices)

    def full_tile(kt, carry):
        tile(kt, False)
        return carry

    lax.fori_loop(0, n_full, full_tile, 0)
    tile(n_full, True)
    o_all = acc_ref[...] / jnp.maximum(l_ref[...], 1e-30)
    lane = lax.broadcasted_iota(jnp.int32, (tq, LANE), 1)
    o = jnp.where(lane < MLA_V, o_all[0:tq], o_all[tq:2 * tq])
    o_ref[...] = (o * _silu(gm_ref[...])).astype(o_ref.dtype)


def _mla_attn(q, k, v, gm, bsz, s_len):
    tq = MLA_TQ
    return pl.pallas_call(
        _mla_attn_kernel,
        grid=(bsz, MLA_HEADS // 2, s_len // tq),
        in_specs=[pl.BlockSpec((None, tq, 2 * LANE), lambda b, hp, i: (b, i, hp)),
                  pl.BlockSpec((None, s_len, 2 * LANE), lambda b, hp, i: (b, 0, hp)),
                  pl.BlockSpec((None, s_len, LANE), lambda b, hp, i: (b, 0, hp)),
                  pl.BlockSpec((None, tq, LANE), lambda b, hp, i: (b, i, hp))],
        out_specs=pl.BlockSpec((None, tq, LANE), lambda b, hp, i: (b, i, hp)),
        out_shape=jax.ShapeDtypeStruct((bsz, s_len, MLA_WIDTH), BF16),
        scratch_shapes=[pltpu.VMEM((2 * tq, LANE), F32)] * 3,
        compiler_params=_cparams(("parallel", "parallel", "parallel")),
        name="mla_attn",
    )(q, k, v, gm)


def _odd_in_layout():
    off = np.cumsum([0, NSA_WIDTH, 6 * NSA_KV_HEADS * NSA_HEAD_DIM, 3 * NSA_HEADS, NSA_WIDTH,
                     MLA_Q_RANK, MLA_KV_RANK, MLA_ROPE, MLA_WIDTH])
    q0, kv0, gl0, gn0, cq0, ckv0, kr0, gm0 = off[:8]
    dh = NSA_HEAD_DIM
    zeros = lambda n: [-1] * n
    cols, segs, start = [], [], 0

    def seg(c):
        nonlocal start
        cols.extend(c)
        segs.append((start, start + len(c)))
        start += len(c)

    lead, tail = zeros(NSA_D0), zeros(LANE - NSA_D0 - dh)
    c = []
    for h in range(NSA_HEADS):
        c += lead + list(range(q0 + h * dh, q0 + (h + 1) * dh)) + tail
    seg(c)
    seg(list(range(kv0, kv0 + NSA_KV_HEADS * dh)))
    seg(list(range(kv0 + NSA_KV_HEADS * dh, kv0 + 2 * NSA_KV_HEADS * dh)))
    c = []
    for slot in range(2, 6):
        for y in range(NSA_KV_HEADS):
            base = kv0 + (slot * NSA_KV_HEADS + y) * dh
            src = list(range(base, base + dh))
            c += (src + src) if slot % 2 == 1 else (lead + src + tail)
    seg(c)
    c = []
    for y in range(NSA_KV_HEADS):
        blk = [gl0 + br * NSA_HEADS + y * NSA_GQA + g for br in range(3) for g in range(NSA_GQA)]
        c += blk + zeros(LANE - len(blk))
    seg(c)
    seg(list(range(gn0, gn0 + NSA_WIDTH)))
    seg(list(range(cq0, cq0 + MLA_Q_RANK)))
    half = MLA_ROPE // 2
    kr = list(range(kr0, kr0 + MLA_ROPE))
    pad_hi = zeros(LANE - MLA_NOPE - MLA_ROPE)
    seg(list(range(ckv0, ckv0 + MLA_KV_RANK))
        + zeros(MLA_NOPE) + kr + pad_hi
        + zeros(MLA_NOPE) + kr[half:] + kr[:half] + pad_hi)
    seg(list(range(gm0, gm0 + MLA_WIDTH)))
    return np.asarray(cols, np.int32), segs


_ODD_COLS, _ODD_SEGS = _odd_in_layout()
_ODD_DTYPES = (BF16, F32, F32, BF16, F32, F32, F32, F32, F32)
_ODD_KVN_SEG = 3
_ODD_COL_SCALE = np.where(np.arange(len(_ODD_COLS)) < _ODD_SEGS[0][1], NSA_HEAD_DIM ** -0.5, 1.0).astype(np.float32)


def _gather_cols(w, cols, col_scale=None):
    cols = np.asarray(cols)
    same_run = ((cols[1:] < 0) & (cols[:-1] < 0)) | ((cols[1:] >= 0) & (cols[:-1] >= 0) & (np.diff(cols) == 1))
    breaks = np.flatnonzero(~same_run) + 1
    parts = []
    for run in np.split(cols, breaks):
        if run[0] < 0:
            parts.append(jnp.zeros((w.shape[0], len(run)), w.dtype))
        else:
            parts.append(w[:, int(run[0]):int(run[0]) + len(run)])
    g = jnp.concatenate(parts, axis=1)
    return g if col_scale is None else g * jnp.asarray(col_scale)[None, :]


def _mla_weight_layout():
    dq = MLA_NOPE + MLA_ROPE
    half = MLA_ROPE // 2
    qa, qb, kn, vv = [], [], [], []
    for h in range(MLA_HEADS):
        nope = list(range(h * dq, h * dq + MLA_NOPE))
        rope = list(range(h * dq + MLA_NOPE, (h + 1) * dq))
        pad = [-1] * (LANE - dq)
        qa += nope + rope + pad
        qb += [-1] * MLA_NOPE + rope[half:] + rope[:half] + pad
        kv0 = h * (MLA_NOPE + MLA_V)
        kn += list(range(kv0, kv0 + MLA_NOPE)) + [-1] * (LANE - MLA_NOPE)
        vv += list(range(kv0 + MLA_NOPE, kv0 + MLA_NOPE + MLA_V))
    return tuple(np.asarray(c, np.int32) for c in (qa, qb, kn, vv))


_MLA_QA, _MLA_QB, _MLA_KN, _MLA_VV = _mla_weight_layout()


def _nsa_constants(s_len):
    n_cmp = (s_len - CMP_LEN) // CMP_STRIDE + 1
    n_sel = s_len // SEL_LEN
    cs = np.arange(LANE)[:, None] * CMP_STRIDE
    sb = np.arange(LANE)[None]
    overlap = ((cs < (sb + 1) * SEL_LEN) & (cs + CMP_LEN > sb * SEL_LEN)
               & (np.arange(LANE)[:, None] < n_cmp) & (sb < n_sel)).astype(np.float32)

    def pos_lanes(pos):
        tab = np.zeros((len(pos), LANE), np.float32)
        tab[:, NSA_POS_HI] = pos // 256
        tab[:, NSA_POS_LO] = pos % 256
        return tab

    t = np.arange(s_len)
    tab_win = pos_lanes(t)
    tab_sel = pos_lanes(t)
    tab_sel[t, t // SEL_LEN] = 1.0
    zero = np.zeros((s_len, LANE), np.float32)
    kvn_tab = np.concatenate([tab_sel, tab_sel, zero, zero, tab_win, tab_win, zero, zero], axis=1)
    cmp_tab = pos_lanes(np.arange(LANE) * CMP_STRIDE + (CMP_LEN - 1))
    return jnp.asarray(overlap, BF16), jnp.asarray(kvn_tab), jnp.asarray(cmp_tab)


_EVEN_SEGS = ((0, 512), (512, 1024), (1024, 2048), (2048, 2560))


def _even_layer(h, h_batch_major, p, layer, out_batch_major, bsz, s_len, pre_g, post_g, w_gate, w_ple,
                w_in, lam_re, lam_im, b_re, b_im, c_re, c_im, d_skip, log_step, w_glu, b_glu,
                w_dw, b_dw, ln_g, ln_b, w_pw, w_out):
    ua, ga, vb, gb = _in_proj(h, h_batch_major, pre_g, w_in.astype(BF16), _EVEN_SEGS, (F32,) * 4, True,
                              bsz, s_len)
    wb, a_re, a_im, wc = _s5_weights(lam_re, lam_im, b_re, b_im, c_re, c_im, log_step)
    ya = _s5(ua, wb, a_re, a_im, wc, d_skip, w_glu.astype(BF16), b_glu, bsz, s_len)
    yb = _conv(vb, w_dw, b_dw, ln_g, ln_b, w_pw.astype(BF16), bsz, s_len)
    return _out_proj(ya, yb, True,
                     h, h_batch_major, p, layer, w_out.astype(BF16), post_g, w_gate.astype(BF16),
                     w_ple.astype(BF16), out_batch_major, bsz, s_len, gates=(ga, gb))


def _odd_layer(h, p, layer, out_batch_major, bsz, s_len, rope_cos, rope_sin, pre_g, post_g, w_gate, w_ple,
               w_in, pos_k, pos_v, ck_w1, ck_w2, cv_w1, cv_w2, q_norm, kv_norm, w_uq, w_ukv, w_out):
    assert s_len // SEL_LEN <= NSA_SEL_LANES and s_len // CMP_STRIDE == LANE
    overlap, kvn_tab, cmp_tab = _nsa_constants(s_len)
    w_in_p = _gather_cols(w_in, _ODD_COLS, _ODD_COL_SCALE).astype(BF16)
    q, kraw, vraw, kvn, gl, gn, cq, ckr, gm = _in_proj(h, False, pre_g, w_in_p, _ODD_SEGS, _ODD_DTYPES, False,
                                                bsz, s_len, _ODD_KVN_SEG, kvn_tab)
    eye = jnp.eye(NSA_KV_HEADS, dtype=F32)

    def w1_blockdiag(w1):
        w = w1.reshape(CMP_LEN, NSA_HEAD_DIM, CMP_HIDDEN)
        return jnp.einsum('ldj,yz->lydzj', w, eye).reshape(
            CMP_LEN, NSA_KV_HEADS * NSA_HEAD_DIM, NSA_KV_HEADS * CMP_HIDDEN).astype(BF16)

    pos_lanes = lambda pos: jnp.tile(pos, (1, NSA_KV_HEADS)).reshape(CMP_LEN, 1, NSA_KV_HEADS * NSA_HEAD_DIM)
    kw2 = jnp.pad(ck_w2, ((0, 0), (NSA_D0, LANE - NSA_D0 - NSA_HEAD_DIM))).astype(BF16)
    vw2 = jnp.concatenate([cv_w2, cv_w2], axis=1).astype(BF16)
    kc, vc = _nsa_compress(kraw, vraw, pos_lanes(pos_k), pos_lanes(pos_v), w1_blockdiag(ck_w1), kw2,
                           w1_blockdiag(cv_w1), vw2, cmp_tab, bsz, s_len)
    y_c = _nsa_attn(q, kc, vc, kvn, gl, gn, overlap, bsz, s_len)
    qm, km, vm = _mla_proj(cq, ckr, rope_cos, rope_sin, q_norm, kv_norm,
                           _gather_cols(w_uq, _MLA_QA).astype(BF16), _gather_cols(w_uq, _MLA_QB).astype(BF16),
                           _gather_cols(w_ukv, _MLA_KN).astype(BF16), _gather_cols(w_ukv, _MLA_VV).astype(BF16),
                           bsz, s_len)
    y_d = _mla_attn(qm, km, vm, gm, bsz, s_len)
    return _out_proj(y_c, y_d, False, h, False, p, layer, w_out.astype(BF16), post_g, w_gate.astype(BF16),
                     w_ple.astype(BF16), out_batch_major, bsz, s_len)


def kernel(x, p, positions, pre_norm, post_norm, ple_gate, ple_proj, ev_w_in, s5_lam_re, s5_lam_im, s5_b_re, s5_b_im, s5_c_re, s5_c_im, s5_d, s5_log_step, s5_w_glu, s5_b_glu, cv_w_dw, cv_b_dw, cv_ln_g, cv_ln_b, cv_w_pw, ev_w_out, od_w_in, nsa_pos_k, nsa_pos_v, nsa_ck_w1, nsa_ck_w2, nsa_cv_w1, nsa_cv_w2, mla_q_norm, mla_kv_norm, mla_w_uq, mla_w_ukv, od_w_out):
    bsz, s_len, _ = x.shape
    assert bsz == SUBLANE and s_len % max(TM, NSA_TK, MLA_TK) == 0
    depth = p.shape[0]
    rope_cos, rope_sin = _rope_tables(positions, bsz, s_len)
    h = x
    for i in range(depth):
        j = i // 2
        last = i == depth - 1
        if i % 2 == 0:
            h = _even_layer(h, i == 0, p, i, last, bsz, s_len, pre_norm[i], post_norm[i], ple_gate[i],
                            ple_proj[i], ev_w_in[j], s5_lam_re[j], s5_lam_im[j], s5_b_re[j], s5_b_im[j],
                            s5_c_re[j], s5_c_im[j], s5_d[j], s5_log_step[j], s5_w_glu[j], s5_b_glu[j],
                            cv_w_dw[j], cv_b_dw[j], cv_ln_g[j], cv_ln_b[j], cv_w_pw[j], ev_w_out[j])
        else:
            h = _odd_layer(h, p, i, last, bsz, s_len, rope_cos, rope_sin, pre_norm[i], post_norm[i],
                           ple_gate[i], ple_proj[i], od_w_in[j], nsa_pos_k[j], nsa_pos_v[j], nsa_ck_w1[j],
                           nsa_ck_w2[j], nsa_cv_w1[j], nsa_cv_w2[j], mla_q_norm[j], mla_kv_norm[j],
                           mla_w_uq[j], mla_w_ukv[j], od_w_out[j])
    return h
```

```python
import functools
import math

import jax
import jax.numpy as jnp
import numpy as np
from jax import lax
from jax.experimental import pallas as pl
from jax.experimental.pallas import tpu as pltpu

F32 = jnp.float32
BF16 = jnp.bfloat16

D_MODEL = 1024
DEPTH = 4
PLE_DIM = 256
EPS = 1e-6
NEG = -1e30

S5_WIDTH = 512
S5_GROUP = 16
S5_GROUPS = 32
S5_STATE = 64
S5_HALF_GROUPS = 16
S5_HALF_STATE = S5_HALF_GROUPS * S5_STATE
CONV_WIDTH = 512
CONV_K = 31

NSA_HEADS = 8
NSA_KV_HEADS = 2
NSA_GQA = 4
NSA_HEAD_DIM = 64
NSA_WIDTH = 512
CMP_LEN = 32
CMP_STRIDE = 16
CMP_HIDDEN = 256
SEL_LEN = 64
SEL_TOPN = 8
WINDOW = 512

MLA_HEADS = 8
MLA_NOPE = 64
MLA_ROPE = 32
MLA_V = 64
MLA_Q_RANK = 256
MLA_KV_RANK = 128
MLA_WIDTH = 512
ROPE_BASE = 10000.0

LANE = 128
SUBLANE = 8
VMEM_LIMIT = 56 * 1024 * 1024

TM = 256
S5_TB = 128
CONV_TB = 128
CONV_ROWS = 32
NSA_TQ = 256
NSA_TK = 512
MLA_TQ = 512
MLA_TK = 512
MLA_SUB_ROWS = 256

NSA_SEL_LANES = 32
NSA_D0 = NSA_SEL_LANES
NSA_POS_HI = NSA_D0 + NSA_HEAD_DIM
NSA_POS_LO = NSA_POS_HI + 1
MASKED = -1e30
M_INIT = -1e29
MLA_SCALE_LOG2 = (MLA_NOPE + MLA_ROPE) ** -0.5 * math.log2(math.e)


def _cparams(sem):
    return pltpu.CompilerParams(dimension_semantics=sem, vmem_limit_bytes=VMEM_LIMIT)


def _dot(a, b):
    return jnp.dot(a, b, preferred_element_type=F32)


def _dot_nt(a, b):
    return lax.dot_general(a, b, (((1,), (1,)), ((), ())), preferred_element_type=F32)


def _rms(x, g):
    return x * lax.rsqrt(jnp.mean(x * x, axis=-1, keepdims=True) + EPS) * g


def _sigmoid(x):
    return 1.0 / (1.0 + jnp.exp(-x))


def _silu(x):
    return x * _sigmoid(x)


def _gelu_tanh(x):
    c = math.sqrt(2.0 / math.pi)
    return 0.5 * x * (1.0 + jnp.tanh(c * (x + 0.044715 * (x * x * x))))


def _masked_softmax(s, mask):
    s = jnp.where(mask, s, NEG)
    m = jnp.max(s, axis=-1, keepdims=True)
    e = jnp.where(mask, jnp.exp(s - m), 0.0)
    return e / jnp.maximum(jnp.sum(e, axis=-1, keepdims=True), 1e-30)


def _flash_tile(scores, v, m_ref, l_ref, acc_ref, row_slices, exp=jnp.exp):
    n_rep = scores[0].shape[1] // LANE
    stats = []
    for s, r in zip(scores, row_slices):
        m_old = m_ref[r, :]
        m_new = jnp.maximum(m_old, jnp.max(s, axis=-1, keepdims=True))
        p = exp(s - jnp.concatenate([m_new] * n_rep, axis=1))
        stats.append((p.astype(BF16), exp(m_old - m_new), jnp.sum(p, axis=-1, keepdims=True), m_new))
    pvs = [_dot(p, v) for p, _, _, _ in stats]
    for (_, alpha, row_sum, m_new), pv, r in zip(stats, pvs, row_slices):
        l_ref[r, :] = alpha * l_ref[r, :] + row_sum
        acc_ref[r, :] = alpha * acc_ref[r, :] + pv
        m_ref[r, :] = m_new


def _in_proj_kernel(segs, tab_seg, x_ref, g_ref, w_ref, *refs):
    tab_ref, out_refs = (None, refs) if tab_seg is None else (refs[0], refs[1:])
    xn = _rms(x_ref[...], g_ref[...]).astype(BF16)
    for n, ((c0, c1), o_ref) in enumerate(zip(segs, out_refs)):
        y = _dot(xn, w_ref[:, c0:c1])
        if n == tab_seg:
            y = y + tab_ref[...]
        o_ref[...] = y.astype(o_ref.dtype)


def _in_proj(h, h_batch_major, gain, w, segs, out_dtypes, out_time_major, bsz, s_len, tab_seg=None, tab=None):
    n_t = s_len // TM
    if h_batch_major:
        x_spec = pl.BlockSpec((None, TM, D_MODEL), lambda b, i: (b, i, 0))
    else:
        x_spec = pl.BlockSpec((TM, D_MODEL), lambda b, i: (i, b))
    out_shapes, out_specs = [], []
    for (c0, c1), dt in zip(segs, out_dtypes):
        n = c1 - c0
        if out_time_major:
            out_shapes.append(jax.ShapeDtypeStruct((s_len, bsz * n), dt))
            out_specs.append(pl.BlockSpec((TM, n), lambda b, i: (i, b)))
        else:
            out_shapes.append(jax.ShapeDtypeStruct((bsz, s_len, n), dt))
            out_specs.append(pl.BlockSpec((None, TM, n), lambda b, i: (b, i, 0)))
    in_specs = [x_spec,
                pl.BlockSpec((1, D_MODEL), lambda b, i: (0, 0)),
                pl.BlockSpec(w.shape, lambda b, i: (0, 0))]
    args = [h, gain.reshape(1, D_MODEL), w]
    if tab_seg is not None:
        in_specs.append(pl.BlockSpec((TM, tab.shape[1]), lambda b, i: (i, 0)))
        args.append(tab)
    return pl.pallas_call(
        functools.partial(_in_proj_kernel, tuple(segs), tab_seg),
        grid=(bsz, n_t),
        in_specs=in_specs,
        out_specs=out_specs,
        out_shape=out_shapes,
        compiler_params=_cparams(("parallel", "parallel")),
        name="in_proj",
    )(*args)


def _out_proj_kernel(gated, ya_ref, yb_ref, *refs):
    if gated:
        ga_ref, gb_ref, h_ref, p_ref, wo_ref, gpost_ref, wg_ref, wp_ref, o_ref = refs
        ya = (ya_ref[...] * _silu(ga_ref[...])).astype(BF16)
        yb = (yb_ref[...] * _silu(gb_ref[...])).astype(BF16)
    else:
        h_ref, p_ref, wo_ref, gpost_ref, wg_ref, wp_ref, o_ref = refs
        ya, yb = ya_ref[...], yb_ref[...]
    half = wo_ref.shape[0] // 2
    y = _dot(ya, wo_ref[0:half, :]) + _dot(yb, wo_ref[half:, :])
    h1 = h_ref[...] + _rms(y, gpost_ref[...])
    gate = _sigmoid(_dot(h1.astype(BF16), wg_ref[...]))
    o_ref[...] = h1 + gate * _dot(p_ref[...].astype(BF16), wp_ref[...])


def _out_proj(ya, yb, y_time_major, h, h_batch_major, p, layer, w_out, g_post, w_gate, w_ple,
              out_batch_major, bsz, s_len, gates=None):
    n_t = s_len // TM
    half = w_out.shape[0] // 2
    if y_time_major:
        y_spec = pl.BlockSpec((TM, half), lambda b, i: (i, b))
    else:
        y_spec = pl.BlockSpec((None, TM, half), lambda b, i: (b, i, 0))
    if h_batch_major:
        h_spec = pl.BlockSpec((None, TM, D_MODEL), lambda b, i: (b, i, 0))
    else:
        h_spec = pl.BlockSpec((TM, D_MODEL), lambda b, i: (i, b))
    if out_batch_major:
        o_shape = jax.ShapeDtypeStruct((bsz, s_len, D_MODEL), F32)
        o_spec = pl.BlockSpec((None, TM, D_MODEL), lambda b, i: (b, i, 0))
    else:
        o_shape = jax.ShapeDtypeStruct((s_len, bsz * D_MODEL), F32)
        o_spec = pl.BlockSpec((TM, D_MODEL), lambda b, i: (i, b))
    const = lambda b, i: (0, 0)
    gated = gates is not None
    return pl.pallas_call(
        functools.partial(_out_proj_kernel, gated),
        grid=(bsz, n_t),
        in_specs=[y_spec, y_spec] + ([y_spec, y_spec] if gated else []) + [
                  h_spec,
                  pl.BlockSpec((None, None, TM, PLE_DIM), lambda b, i: (layer, b, i, 0)),
                  pl.BlockSpec(w_out.shape, const),
                  pl.BlockSpec((1, D_MODEL), const),
                  pl.BlockSpec(w_gate.shape, const),
                  pl.BlockSpec(w_ple.shape, const)],
        out_specs=o_spec,
        out_shape=o_shape,
        compiler_params=_cparams(("parallel", "parallel")),
        name="out_proj",
    )(ya, yb, *(gates or ()), h, p, w_out, g_post.reshape(1, D_MODEL), w_gate, w_ple)


def _rows_from_batch_cols(src_ref, slab_ref, width):
    steps = src_ref.shape[0]
    for b in range(SUBLANE):
        for j in range(width // LANE):
            c0 = b * width + j * LANE
            slab_ref[j, pl.ds(b, steps, stride=SUBLANE), :] = src_ref[:, c0:c0 + LANE]


def _batch_cols_from_rows(slab_ref, dst_ref, width):
    steps = dst_ref.shape[0]
    for b in range(SUBLANE):
        for j in range(width // LANE):
            c0 = b * width + j * LANE
            dst_ref[:, c0:c0 + LANE] = slab_ref[j, pl.ds(b, steps, stride=SUBLANE), :].astype(dst_ref.dtype)


def _s5_kernel(u_ref, wb_ref, are_ref, aim_ref, wc_ref, d_ref, wglu_ref, bglu_ref,
               o_ref, buf_ref, st_ref, slab_ref):
    n_steps = u_ref.shape[0]
    rows = n_steps * SUBLANE
    hs = S5_HALF_STATE
    n_slabs = S5_WIDTH // LANE

    @pl.when(pl.program_id(0) == 0)
    def _():
        st_ref[...] = jnp.zeros_like(st_ref)

    _rows_from_batch_cols(u_ref, slab_ref, S5_WIDTH)
    u = jnp.concatenate([slab_ref[j] for j in range(n_slabs)], axis=-1)
    ub = u.astype(BF16)
    ys = []
    for hf in range(2):
        c0 = hf * 2 * hs
        buf_ref[:, c0:c0 + 2 * hs] = _dot(ub[:, hf * 256:(hf + 1) * 256], wb_ref[hf])
        a_re = jnp.broadcast_to(are_ref[hf], (SUBLANE, hs))
        a_im = jnp.broadcast_to(aim_ref[hf], (SUBLANE, hs))

        def step(t, carry, c0=c0, a_re=a_re, a_im=a_im):
            x_re, x_im = carry
            r0 = pl.multiple_of(t * SUBLANE, SUBLANE)
            b_re = buf_ref[pl.ds(r0, SUBLANE), c0:c0 + hs]
            b_im = buf_ref[pl.ds(r0, SUBLANE), c0 + hs:c0 + 2 * hs]
            n_re = a_re * x_re - a_im * x_im + b_re
            n_im = a_re * x_im + a_im * x_re + b_im
            buf_ref[pl.ds(r0, SUBLANE), c0:c0 + hs] = n_re
            buf_ref[pl.ds(r0, SUBLANE), c0 + hs:c0 + 2 * hs] = n_im
            return n_re, n_im

        x_re, x_im = lax.fori_loop(
            0, n_steps, step, (st_ref[:, c0:c0 + hs], st_ref[:, c0 + hs:c0 + 2 * hs]))
        st_ref[:, c0:c0 + hs] = x_re
        st_ref[:, c0 + hs:c0 + 2 * hs] = x_im
        ys.append(_dot(buf_ref[:, c0:c0 + 2 * hs].astype(BF16), wc_ref[hf]))
    y = jnp.concatenate(ys, axis=-1) + d_ref[...] * u
    y = _gelu_tanh(y)
    y = y * _sigmoid(_dot(y.astype(BF16), wglu_ref[...]) + bglu_ref[...])
    for j in range(n_slabs):
        slab_ref[j] = y[:, j * LANE:(j + 1) * LANE]
    _batch_cols_from_rows(slab_ref, o_ref, S5_WIDTH)


def _s5(u, wb, a_re, a_im, wc, d_skip, w_glu, b_glu, bsz, s_len):
    rows = S5_TB * bsz
    n_blk = s_len // S5_TB
    const2 = lambda i: (0, 0)
    const3 = lambda i: (0, 0, 0)
    return pl.pallas_call(
        _s5_kernel,
        grid=(n_blk,),
        in_specs=[pl.BlockSpec((S5_TB, bsz * S5_WIDTH), lambda i: (i, 0)),
                  pl.BlockSpec(wb.shape, const3),
                  pl.BlockSpec(a_re.shape, const3),
                  pl.BlockSpec(a_im.shape, const3),
                  pl.BlockSpec(wc.shape, const3),
                  pl.BlockSpec((1, S5_WIDTH), const2),
                  pl.BlockSpec(w_glu.shape, const2),
                  pl.BlockSpec((1, S5_WIDTH), const2)],
        out_specs=pl.BlockSpec((S5_TB, bsz * S5_WIDTH), lambda i: (i, 0)),
        out_shape=jax.ShapeDtypeStruct((s_len, bsz * S5_WIDTH), BF16),
        scratch_shapes=[pltpu.VMEM((rows, 4 * S5_HALF_STATE), F32),
                        pltpu.VMEM((SUBLANE, 4 * S5_HALF_STATE), F32),
                        pltpu.VMEM((S5_WIDTH // LANE, rows, LANE), F32)],
        compiler_params=_cparams(("arbitrary",)),
        name="s5",
    )(u, wb, a_re, a_im, wc, d_skip.reshape(1, S5_WIDTH), w_glu, b_glu.reshape(1, S5_WIDTH))


def _s5_weights(lam_re, lam_im, b_re, b_im, c_re, c_im, log_step):
    step = jnp.exp(log_step.astype(F32))[:, None]
    lr, li = lam_re.astype(F32), lam_im.astype(F32)
    mag = jnp.exp(lr * step)
    ab_re, ab_im = mag * jnp.cos(li * step), mag * jnp.sin(li * step)
    den = lr * lr + li * li
    nr, ni = ab_re - 1.0, ab_im
    f_re, f_im = (nr * lr + ni * li) / den, (ni * lr - nr * li) / den
    br, bi = b_re.astype(F32), b_im.astype(F32)
    bb_re = f_re[..., None] * br - f_im[..., None] * bi
    bb_im = f_re[..., None] * bi + f_im[..., None] * br
    hg = S5_HALF_GROUPS
    eye = jnp.eye(hg, dtype=F32)

    def b_blockdiag(bb):
        bb = bb.reshape(2, hg, S5_STATE, S5_GROUP)
        return jnp.einsum('hgnc,gk->hgckn', bb, eye).reshape(2, hg * S5_GROUP, hg * S5_STATE)

    def c_blockdiag(cc):
        cc = cc.reshape(2, hg, S5_GROUP, S5_STATE)
        return jnp.einsum('hgcn,gk->hgnkc', cc, eye).reshape(2, hg * S5_STATE, hg * S5_GROUP)

    wb = jnp.concatenate([b_blockdiag(bb_re), b_blockdiag(bb_im)], axis=-1).astype(BF16)
    wc = jnp.concatenate([c_blockdiag(c_re.astype(F32)), -c_blockdiag(c_im.astype(F32))],
                         axis=1).astype(BF16)
    a_re = ab_re.reshape(2, 1, S5_HALF_STATE)
    a_im = ab_im.reshape(2, 1, S5_HALF_STATE)
    return wb, a_re, a_im, wc


def _conv_kernel(v_ref, wdw_ref, bdw_ref, lng_ref, lnb_ref, wpw_ref, o_ref, hbuf_ref, cbuf_ref, slab_ref):
    rows = v_ref.shape[0] * SUBLANE
    halo = (CONV_K - 1) * SUBLANE
    n_slabs = CONV_WIDTH // LANE

    @pl.when(pl.program_id(0) == 0)
    def _():
        hbuf_ref[0:halo, :] = jnp.zeros((halo, CONV_WIDTH), F32)

    @pl.when(pl.program_id(0) > 0)
    def _():
        hbuf_ref[0:halo, :] = hbuf_ref[rows:rows + halo, :]

    _rows_from_batch_cols(v_ref, slab_ref, 2 * CONV_WIDTH)
    for j in range(n_slabs):
        hbuf_ref[halo:halo + rows, j * LANE:(j + 1) * LANE] = slab_ref[j] * _sigmoid(slab_ref[n_slabs + j])

    def chunk(c, carry):
        r0 = pl.multiple_of(c * CONV_ROWS, CONV_ROWS)
        acc = jnp.zeros((CONV_ROWS, CONV_WIDTH), F32)
        for k in range(CONV_K):
            acc = acc + wdw_ref[k:k + 1, :] * hbuf_ref[pl.ds(r0 + k * SUBLANE, CONV_ROWS), :]
        cbuf_ref[pl.ds(r0, CONV_ROWS), :] = acc
        return carry

    lax.fori_loop(0, rows // CONV_ROWS, chunk, 0)
    c = cbuf_ref[...] + bdw_ref[...]
    mu = jnp.mean(c, axis=-1, keepdims=True)
    var = jnp.mean(jnp.square(c - mu), axis=-1, keepdims=True)
    hn = _silu((c - mu) * lax.rsqrt(var + EPS) * lng_ref[...] + lnb_ref[...])
    y = _dot(hn.astype(BF16), wpw_ref[...])
    for j in range(n_slabs):
        slab_ref[j] = y[:, j * LANE:(j + 1) * LANE]
    _batch_cols_from_rows(slab_ref, o_ref, CONV_WIDTH)


def _conv(v, w_dw, b_dw, ln_g, ln_b, w_pw, bsz, s_len):
    rows = CONV_TB * bsz
    n_blk = s_len // CONV_TB
    halo = (CONV_K - 1) * SUBLANE
    const2 = lambda i: (0, 0)
    vec = lambda a: a.reshape(1, CONV_WIDTH)
    return pl.pallas_call(
        _conv_kernel,
        grid=(n_blk,),
        in_specs=[pl.BlockSpec((CONV_TB, bsz * 2 * CONV_WIDTH), lambda i: (i, 0)),
                  pl.BlockSpec((CONV_K, CONV_WIDTH), const2),
                  pl.BlockSpec((1, CONV_WIDTH), const2),
                  pl.BlockSpec((1, CONV_WIDTH), const2),
                  pl.BlockSpec((1, CONV_WIDTH), const2),
                  pl.BlockSpec(w_pw.shape, const2)],
        out_specs=pl.BlockSpec((CONV_TB, bsz * CONV_WIDTH), lambda i: (i, 0)),
        out_shape=jax.ShapeDtypeStruct((s_len, bsz * CONV_WIDTH), BF16),
        scratch_shapes=[pltpu.VMEM((halo + rows, CONV_WIDTH), F32),
                        pltpu.VMEM((rows, CONV_WIDTH), F32),
                        pltpu.VMEM((2 * CONV_WIDTH // LANE, rows, LANE), F32)],
        compiler_params=_cparams(("arbitrary",)),
        name="conv",
    )(v, w_dw, vec(b_dw), vec(ln_g), vec(ln_b), w_pw)


def _nsa_compress_kernel(kraw_ref, vraw_ref, pk_ref, pv_ref, kw1_ref, kw2_ref, vw1_ref, vw2_ref, ktab_ref,
                         ko_ref, vo_ref):
    n_chunks = ko_ref.shape[1]

    def compress(raw_ref, pos_ref, w1_ref, w2_ref, o_ref, tab):
        lo = jnp.zeros((n_chunks, NSA_KV_HEADS * CMP_HIDDEN), F32)
        hi = jnp.zeros((n_chunks, NSA_KV_HEADS * CMP_HIDDEN), F32)
        for l in range(CMP_STRIDE):
            x = raw_ref[pl.ds(l, n_chunks, stride=CMP_STRIDE), :]
            lo = lo + _dot((x + pos_ref[l]).astype(BF16), w1_ref[l])
            hi = hi + _dot((x + pos_ref[CMP_STRIDE + l]).astype(BF16), w1_ref[CMP_STRIDE + l])
        hid = _gelu_tanh(lo + pltpu.roll(hi, n_chunks - 1, 0))
        for y in range(NSA_KV_HEADS):
            out = _dot(hid[:, y * CMP_HIDDEN:(y + 1) * CMP_HIDDEN].astype(BF16), w2_ref[...])
            o_ref[y] = (out if tab is None else out + tab).astype(o_ref.dtype)

    compress(kraw_ref, pk_ref, kw1_ref, kw2_ref, ko_ref, ktab_ref[...])
    compress(vraw_ref, pv_ref, vw1_ref, vw2_ref, vo_ref, None)


def _nsa_compress(kraw, vraw, pos_k, pos_v, kw1, kw2, vw1, vw2, ktab, bsz, s_len):
    n_chunks = s_len // CMP_STRIDE
    const2 = lambda b: (0, 0)
    const3 = lambda b: (0, 0, 0)
    raw_spec = pl.BlockSpec((None, s_len, LANE), lambda b: (b, 0, 0))
    o_shape = jax.ShapeDtypeStruct((bsz, NSA_KV_HEADS, n_chunks, LANE), BF16)
    o_spec = pl.BlockSpec((None, NSA_KV_HEADS, n_chunks, LANE), lambda b: (b, 0, 0, 0))
    return pl.pallas_call(
        _nsa_compress_kernel,
        grid=(bsz,),
        in_specs=[raw_spec, raw_spec,
                  pl.BlockSpec(pos_k.shape, const3),
                  pl.BlockSpec(pos_v.shape, const3),
                  pl.BlockSpec(kw1.shape, const3),
                  pl.BlockSpec(kw2.shape, const2),
                  pl.BlockSpec(vw1.shape, const3),
                  pl.BlockSpec(vw2.shape, const2),
                  pl.BlockSpec(ktab.shape, const2)],
        out_specs=[o_spec, o_spec],
        out_shape=[o_shape, o_shape],
        compiler_params=_cparams(("parallel",)),
        name="nsa_compress",
    )(kraw, vraw, pos_k, pos_v, kw1, kw2, vw1, vw2, ktab)


def _nsa_attn_kernel(s_len, q_ref, kc_ref, vc_ref, ks_ref, vs_ref, kw_ref, vw_ref, gl_ref, gn_ref,
                     ov_ref, o_ref, m_ref, l_ref, acc_ref):
    tq, tk = NSA_TQ, NSA_TK
    y = pl.program_id(1)
    t0 = pl.program_id(2) * tq
    n_cmp = (s_len - CMP_LEN) // CMP_STRIDE + 1
    n_sel_blocks = s_len // SEL_LEN
    n_top = min(SEL_TOPN, n_sel_blocks)
    lane_i = lax.broadcasted_iota(jnp.int32, (tq, LANE), 1)
    tpos = t0 + lax.broadcasted_iota(jnp.int32, (tq, LANE), 0)
    slopes = [jnp.where(y == 0, 2.0 ** -(g + 1), 2.0 ** -(NSA_GQA + g + 1)).astype(F32)
              for g in range(NSA_GQA)]

    heads = range(NSA_GQA)

    def q_heads(extra):
        parts = []
        for g in heads:
            aug = jnp.where(lane_i == NSA_POS_HI, slopes[g] * 256.0,
                            jnp.where(lane_i == NSA_POS_LO, slopes[g], 0.0))
            if extra is not None:
                aug = aug + extra
            parts.append(q_ref[:, g * LANE:(g + 1) * LANE] + aug.astype(BF16))
        return parts

    mask_c = (tpos >= lane_i * CMP_STRIDE + (CMP_LEN - 1)) & (lane_i < n_cmp)
    kc, vc = kc_ref[...], vc_ref[...]
    p_c = [_masked_softmax(_dot_nt(q, kc), mask_c) for q in q_heads(None)]
    o_cmp = [_dot(p.astype(BF16), vc) for p in p_c]

    p_sum = (p_c[0] + p_c[1]) + (p_c[2] + p_c[3])
    p_hi = p_sum.astype(BF16)
    p_lo = (p_sum - p_hi.astype(F32)).astype(BF16)
    p_slc = _dot(p_hi, ov_ref[...]) + _dot(p_lo, ov_ref[...])
    cur = tpos // SEL_LEN
    forced = (lane_i == 0) | (lane_i == cur) | (lane_i == cur - 1)
    causal_blk = lane_i * SEL_LEN <= tpos
    score = jnp.where(forced, p_slc + 1e4, jnp.where(causal_blk, p_slc, -1e4))
    score_t = score.T[0:n_sel_blocks, :]
    j_idx = lax.broadcasted_iota(jnp.int32, score_t.shape, 0)
    rank = jnp.zeros(score_t.shape, F32)
    for blk in range(n_sel_blocks):
        row = score_t[blk:blk + 1, :]
        beats = (row > score_t) | ((row == score_t) & (j_idx > blk))
        rank = rank + jnp.where(beats, 1.0, 0.0)
    bias_t = jnp.where(rank < n_top, 0.0, MASKED)
    sel_bias = jnp.concatenate([bias_t, jnp.zeros((LANE - n_sel_blocks, tq), F32)], axis=0).T

    qs = q_heads(sel_bias)
    m_ref[...] = jnp.full_like(m_ref, M_INIT)
    l_ref[...] = jnp.zeros_like(l_ref)
    acc_ref[...] = jnp.zeros_like(acc_ref)

    def sel_tile(kt, causal_mask):
        k0 = pl.multiple_of(kt * tk, tk)
        k = ks_ref[pl.ds(k0, tk), :]
        v = vs_ref[pl.ds(k0, tk), :]
        scores = [_dot_nt(qs[g], k) for g in heads]
        if causal_mask:
            ok = (k0 + lax.broadcasted_iota(jnp.int32, (tq, tk), 1)
                  <= t0 + lax.broadcasted_iota(jnp.int32, (tq, tk), 0))
            scores = [jnp.where(ok, s, MASKED) for s in scores]
        _flash_tile(scores, v, m_ref, l_ref, acc_ref, [slice(g * tq, (g + 1) * tq) for g in heads])

    n_full = t0 // tk

    def full_tile(kt, carry):
        sel_tile(kt, False)
        return carry

    lax.fori_loop(0, n_full, full_tile, 0)
    sel_tile(n_full, True)

    span = min(WINDOW + tq, s_len)
    ws = pl.multiple_of(jnp.minimum(jnp.maximum(t0 - WINDOW, 0), s_len - span), tq)
    dist_w = ((t0 - ws) + lax.broadcasted_iota(jnp.int32, (tq, span), 0)
              - lax.broadcasted_iota(jnp.int32, (tq, span), 1))
    bias_w = jnp.where((dist_w >= 0) & (dist_w < WINDOW), 0.0, MASKED)
    kw = kw_ref[pl.ds(ws, span), :]
    vw = vw_ref[pl.ds(ws, span), :]
    s_w = [_dot_nt(qs[g], kw) + bias_w for g in heads]
    e_w = [jnp.exp(s - jnp.max(s, axis=-1, keepdims=True)) for s in s_w]
    sum_w = [jnp.maximum(jnp.sum(e, axis=-1, keepdims=True), 1e-30) for e in e_w]
    o_win = [_dot(e.astype(BF16), vw) / d for e, d in zip(e_w, sum_w)]

    gates = _sigmoid(gl_ref[...])
    outs = []
    for g in heads:
        r = slice(g * tq, (g + 1) * tq)
        o_sel = acc_ref[r, :] / jnp.maximum(l_ref[r, :], 1e-30)
        outs.append(gates[:, g:g + 1] * o_cmp[g]
                    + gates[:, NSA_GQA + g:NSA_GQA + g + 1] * o_sel
                    + gates[:, 2 * NSA_GQA + g:2 * NSA_GQA + g + 1] * o_win[g])
    lo = lane_i < NSA_HEAD_DIM
    o = jnp.concatenate([jnp.where(lo, outs[0], outs[1]), jnp.where(lo, outs[2], outs[3])], axis=-1)
    o_ref[...] = (o * _silu(gn_ref[...])).astype(o_ref.dtype)


def _nsa_attn(q, kc, vc, kvn, gl, gn, overlap, bsz, s_len):
    tq = NSA_TQ
    n_chunks = kc.shape[2]
    rows = NSA_GQA * tq
    kv_spec = lambda col: pl.BlockSpec((None, s_len, LANE), lambda b, y, i, col=col: (b, 0, col + y))
    c_spec = pl.BlockSpec((None, None, n_chunks, LANE), lambda b, y, i: (b, y, 0, 0))
    return pl.pallas_call(
        functools.partial(_nsa_attn_kernel, s_len),
        grid=(bsz, NSA_KV_HEADS, s_len // tq),
        in_specs=[pl.BlockSpec((None, tq, NSA_GQA * LANE), lambda b, y, i: (b, i, y)),
                  c_spec, c_spec,
                  kv_spec(0), kv_spec(2), kv_spec(4), kv_spec(6),
                  pl.BlockSpec((None, tq, LANE), lambda b, y, i: (b, i, y)),
                  pl.BlockSpec((None, tq, NSA_GQA * NSA_HEAD_DIM), lambda b, y, i: (b, i, y)),
                  pl.BlockSpec(overlap.shape, lambda b, y, i: (0, 0))],
        out_specs=pl.BlockSpec((None, tq, NSA_GQA * NSA_HEAD_DIM), lambda b, y, i: (b, i, y)),
        out_shape=jax.ShapeDtypeStruct((bsz, s_len, NSA_WIDTH), BF16),
        scratch_shapes=[pltpu.VMEM((rows, LANE), F32)] * 3,
        compiler_params=_cparams(("parallel", "parallel", "parallel")),
        name="nsa_attn",
    )(q, kc, vc, kvn, kvn, kvn, kvn, gl, gn, overlap)


def _rope_table_kernel(pos_ref, freq_ref, cos_ref, sin_ref):
    ang = pos_ref[...].astype(F32) * freq_ref[...]
    lane = lax.broadcasted_iota(jnp.int32, ang.shape, 1)
    half = MLA_ROPE // 2
    first = (lane >= MLA_NOPE) & (lane < MLA_NOPE + half)
    second = (lane >= MLA_NOPE + half) & (lane < MLA_NOPE + MLA_ROPE)
    cos_ref[...] = jnp.where(first | second, jnp.cos(ang), jnp.where(lane < MLA_NOPE, 1.0, 0.0))
    sn = jnp.sin(ang)
    sin_ref[...] = jnp.where(first, -sn, jnp.where(second, sn, 0.0))


def _rope_tables(positions, bsz, s_len):
    half = MLA_ROPE // 2
    freqs = ROPE_BASE ** (-jnp.arange(half, dtype=F32) / half)
    freq_lane = jnp.concatenate([jnp.zeros((MLA_NOPE,), F32), freqs, freqs,
                                 jnp.zeros((LANE - MLA_NOPE - MLA_ROPE,), F32)]).reshape(1, LANE)
    spec = pl.BlockSpec((None, TM, LANE), lambda b, i: (b, i, 0))
    shape = jax.ShapeDtypeStruct((bsz, s_len, LANE), F32)
    return pl.pallas_call(
        _rope_table_kernel,
        grid=(bsz, s_len // TM),
        in_specs=[pl.BlockSpec((None, TM, 1), lambda b, i: (b, i, 0)),
                  pl.BlockSpec((1, LANE), lambda b, i: (0, 0))],
        out_specs=[spec, spec],
        out_shape=[shape, shape],
        compiler_params=_cparams(("parallel", "parallel")),
        name="rope_tables",
    )(positions.reshape(bsz, s_len, 1), freq_lane)


def _mla_proj_kernel(cq_ref, ckr_ref, cos_ref, sin_ref, qn_ref, kvn_ref, wqa_ref, wqb_ref, wk_ref, wv_ref,
                     q_ref, k_ref, v_ref):
    cq = _rms(cq_ref[...], qn_ref[...]).astype(BF16)
    ckv = _rms(ckr_ref[:, 0:MLA_KV_RANK], kvn_ref[...]).astype(BF16)
    cosm = cos_ref[...]
    sinm = sin_ref[...]
    kr = ckr_ref[:, MLA_KV_RANK:MLA_KV_RANK + LANE] * cosm + ckr_ref[:, MLA_KV_RANK + LANE:] * sinm
    v_ref[...] = _dot(ckv, wv_ref[...]).astype(v_ref.dtype)
    for h in range(MLA_HEADS):
        c = slice(h * LANE, (h + 1) * LANE)
        q = _dot(cq, wqa_ref[:, c]) * cosm + _dot(cq, wqb_ref[:, c]) * sinm
        q_ref[:, c] = (q * MLA_SCALE_LOG2).astype(q_ref.dtype)
        k_ref[:, c] = (_dot(ckv, wk_ref[:, c]) + kr).astype(k_ref.dtype)


def _mla_proj(cq, ckr, cosm, sinm, q_norm, kv_norm, wqa, wqb, wk, wv, bsz, s_len):
    const = lambda b, i: (0, 0)
    tok = lambda n: pl.BlockSpec((None, TM, n), lambda b, i: (b, i, 0))
    return pl.pallas_call(
        _mla_proj_kernel,
        grid=(bsz, s_len // TM),
        in_specs=[tok(MLA_Q_RANK), tok(MLA_KV_RANK + 2 * LANE), tok(LANE), tok(LANE),
                  pl.BlockSpec((1, MLA_Q_RANK), const), pl.BlockSpec((1, MLA_KV_RANK), const),
                  pl.BlockSpec(wqa.shape, const), pl.BlockSpec(wqb.shape, const),
                  pl.BlockSpec(wk.shape, const), pl.BlockSpec(wv.shape, const)],
        out_specs=[tok(MLA_HEADS * LANE), tok(MLA_HEADS * LANE), tok(MLA_WIDTH)],
        out_shape=[jax.ShapeDtypeStruct((bsz, s_len, MLA_HEADS * LANE), BF16),
                   jax.ShapeDtypeStruct((bsz, s_len, MLA_HEADS * LANE), BF16),
                   jax.ShapeDtypeStruct((bsz, s_len, MLA_WIDTH), BF16)],
        compiler_params=_cparams(("parallel", "parallel")),
        name="mla_proj",
    )(cq, ckr, cosm, sinm, q_norm.reshape(1, MLA_Q_RANK), kv_norm.reshape(1, MLA_KV_RANK), wqa, wqb, wk, wv)


def _mla_attn_kernel(q_ref, k_ref, v_ref, gm_ref, o_ref, m_ref, l_ref, acc_ref):
    tq, tk = MLA_TQ, MLA_TK
    t0 = pl.program_id(2) * tq
    n_full = t0 // tk
    sub = MLA_SUB_ROWS
    n_sub = tq // sub
    m_ref[...] = jnp.full_like(m_ref, M_INIT)
    l_ref[...] = jnp.zeros_like(l_ref)
    acc_ref[...] = jnp.zeros_like(acc_ref)

    def tile(kt, causal_mask):
        k0 = pl.multiple_of(kt * tk, tk)
        v = v_ref[pl.ds(k0, tk), :]
        scores, row_slices = [], []
        for hh in range(2):
            k = k_ref[pl.ds(k0, tk), hh * LANE:(hh + 1) * LANE]
            for rb in range(n_sub):
                s = _dot_nt(q_ref[rb * sub:(rb + 1) * sub, hh * LANE:(hh + 1) * LANE], k)
                if causal_mask:
                    ok = (k0 + lax.broadcasted_iota(jnp.int32, (sub, tk), 1)
                          <= t0 + rb * sub + lax.broadcasted_iota(jnp.int32, (sub, tk), 0))
                    s = jnp.where(ok, s, MASKED)
                scores.append(s)
                row_slices.append(slice(hh * tq + rb * sub, hh * tq + (rb + 1) * sub))
        _flash_tile(scores, v, m_ref, l_ref, acc_ref, row_slices, exp=jnp.exp2)

    def full_tile(kt, carry):
        tile(kt, False)
        return carry

    lax.fori_loop(0, n_full, full_tile, 0)
    tile(n_full, True)
    o_all = acc_ref[...] / jnp.maximum(l_ref[...], 1e-30)
    lane = lax.broadcasted_iota(jnp.int32, (tq, LANE), 1)
    o = jnp.where(lane < MLA_V, o_all[0:tq], o_all[tq:2 * tq])
    o_ref[...] = (o * _silu(gm_ref[...])).astype(o_ref.dtype)


def _mla_attn(q, k, v, gm, bsz, s_len):
    tq = MLA_TQ
    return pl.pallas_call(
        _mla_attn_kernel,
        grid=(bsz, MLA_HEADS // 2, s_len // tq),
        in_specs=[pl.BlockSpec((None, tq, 2 * LANE), lambda b, hp, i: (b, i, hp)),
                  pl.BlockSpec((None, s_len, 2 * LANE), lambda b, hp, i: (b, 0, hp)),
                  pl.BlockSpec((None, s_len, LANE), lambda b, hp, i: (b, 0, hp)),
                  pl.BlockSpec((None, tq, LANE), lambda b, hp, i: (b, i, hp))],
        out_specs=pl.BlockSpec((None, tq, LANE), lambda b, hp, i: (b, i, hp)),
        out_shape=jax.ShapeDtypeStruct((bsz, s_len, MLA_WIDTH), BF16),
        scratch_shapes=[pltpu.VMEM((2 * tq, LANE), F32)] * 3,
        compiler_params=_cparams(("parallel", "parallel", "parallel")),
        name="mla_attn",
    )(q, k, v, gm)


def _odd_in_layout():
    off = np.cumsum([0, NSA_WIDTH, 6 * NSA_KV_HEADS * NSA_HEAD_DIM, 3 * NSA_HEADS, NSA_WIDTH,
                     MLA_Q_RANK, MLA_KV_RANK, MLA_ROPE, MLA_WIDTH])
    q0, kv0, gl0, gn0, cq0, ckv0, kr0, gm0 = off[:8]
    dh = NSA_HEAD_DIM
    zeros = lambda n: [-1] * n
    cols, segs, start = [], [], 0

    def seg(c):
        nonlocal start
        cols.extend(c)
        segs.append((start, start + len(c)))
        start += len(c)

    lead, tail = zeros(NSA_D0), zeros(LANE - NSA_D0 - dh)
    c = []
    for h in range(NSA_HEADS):
        c += lead + list(range(q0 + h * dh, q0 + (h + 1) * dh)) + tail
    seg(c)
    seg(list(range(kv0, kv0 + NSA_KV_HEADS * dh)))
    seg(list(range(kv0 + NSA_KV_HEADS * dh, kv0 + 2 * NSA_KV_HEADS * dh)))
    c = []
    for slot in range(2, 6):
        for y in range(NSA_KV_HEADS):
            base = kv0 + (slot * NSA_KV_HEADS + y) * dh
            src = list(range(base, base + dh))
            c += (src + src) if slot % 2 == 1 else (lead + src + tail)
    seg(c)
    c = []
    for y in range(NSA_KV_HEADS):
        blk = [gl0 + br * NSA_HEADS + y * NSA_GQA + g for br in range(3) for g in range(NSA_GQA)]
        c += blk + zeros(LANE - len(blk))
    seg(c)
    seg(list(range(gn0, gn0 + NSA_WIDTH)))
    seg(list(range(cq0, cq0 + MLA_Q_RANK)))
    half = MLA_ROPE // 2
    kr = list(range(kr0, kr0 + MLA_ROPE))
    pad_hi = zeros(LANE - MLA_NOPE - MLA_ROPE)
    seg(list(range(ckv0, ckv0 + MLA_KV_RANK))
        + zeros(MLA_NOPE) + kr + pad_hi
        + zeros(MLA_NOPE) + kr[half:] + kr[:half] + pad_hi)
    seg(list(range(gm0, gm0 + MLA_WIDTH)))
    return np.asarray(cols, np.int32), segs


_ODD_COLS, _ODD_SEGS = _odd_in_layout()
_ODD_DTYPES = (BF16, F32, F32, BF16, F32, F32, F32, F32, F32)
_ODD_KVN_SEG = 3
_ODD_COL_SCALE = np.where(np.arange(len(_ODD_COLS)) < _ODD_SEGS[0][1], NSA_HEAD_DIM ** -0.5, 1.0).astype(np.float32)


def _gather_cols(w, cols, col_scale=None):
    cols = np.asarray(cols)
    same_run = ((cols[1:] < 0) & (cols[:-1] < 0)) | ((cols[1:] >= 0) & (cols[:-1] >= 0) & (np.diff(cols) == 1))
    breaks = np.flatnonzero(~same_run) + 1
    parts = []
    for run in np.split(cols, breaks):
        if run[0] < 0:
            parts.append(jnp.zeros((w.shape[0], len(run)), w.dtype))
        else:
            parts.append(w[:, int(run[0]):int(run[0]) + len(run)])
    g = jnp.concatenate(parts, axis=1)
    return g if col_scale is None else g * jnp.asarray(col_scale)[None, :]


def _mla_weight_layout():
    dq = MLA_NOPE + MLA_ROPE
    half = MLA_ROPE // 2
    qa, qb, kn, vv = [], [], [], []
    for h in range(MLA_HEADS):
        nope = list(range(h * dq, h * dq + MLA_NOPE))
        rope = list(range(h * dq + MLA_NOPE, (h + 1) * dq))
        pad = [-1] * (LANE - dq)
        qa += nope + rope + pad
        qb += [-1] * MLA_NOPE + rope[half:] + rope[:half] + pad
        kv0 = h * (MLA_NOPE + MLA_V)
        kn += list(range(kv0, kv0 + MLA_NOPE)) + [-1] * (LANE - MLA_NOPE)
        vv += list(range(kv0 + MLA_NOPE, kv0 + MLA_NOPE + MLA_V))
    return tuple(np.asarray(c, np.int32) for c in (qa, qb, kn, vv))


_MLA_QA, _MLA_QB, _MLA_KN, _MLA_VV = _mla_weight_layout()


def _nsa_constants(s_len):
    n_cmp = (s_len - CMP_LEN) // CMP_STRIDE + 1
    n_sel = s_len // SEL_LEN
    cs = np.arange(LANE)[:, None] * CMP_STRIDE
    sb = np.arange(LANE)[None]
    overlap = ((cs < (sb + 1) * SEL_LEN) & (cs + CMP_LEN > sb * SEL_LEN)
               & (np.arange(LANE)[:, None] < n_cmp) & (sb < n_sel)).astype(np.float32)

    def pos_lanes(pos):
        tab = np.zeros((len(pos), LANE), np.float32)
        tab[:, NSA_POS_HI] = pos // 256
        tab[:, NSA_POS_LO] = pos % 256
        return tab

    t = np.arange(s_len)
    tab_win = pos_lanes(t)
    tab_sel = pos_lanes(t)
    tab_sel[t, t // SEL_LEN] = 1.0
    zero = np.zeros((s_len, LANE), np.float32)
    kvn_tab = np.concatenate([tab_sel, tab_sel, zero, zero, tab_win, tab_win, zero, zero], axis=1)
    cmp_tab = pos_lanes(np.arange(LANE) * CMP_STRIDE + (CMP_LEN - 1))
    return jnp.asarray(overlap, BF16), jnp.asarray(kvn_tab), jnp.asarray(cmp_tab)


_EVEN_SEGS = ((0, 512), (512, 1024), (1024, 2048), (2048, 2560))


def _even_layer(h, h_batch_major, p, layer, out_batch_major, bsz, s_len, pre_g, post_g, w_gate, w_ple,
                w_in, lam_re, lam_im, b_re, b_im, c_re, c_im, d_skip, log_step, w_glu, b_glu,
                w_dw, b_dw, ln_g, ln_b, w_pw, w_out):
    ua, ga, vb, gb = _in_proj(h, h_batch_major, pre_g, w_in.astype(BF16), _EVEN_SEGS, (F32,) * 4, True,
                              bsz, s_len)
    wb, a_re, a_im, wc = _s5_weights(lam_re, lam_im, b_re, b_im, c_re, c_im, log_step)
    ya = _s5(ua, wb, a_re, a_im, wc, d_skip, w_glu.astype(BF16), b_glu, bsz, s_len)
    yb = _conv(vb, w_dw, b_dw, ln_g, ln_b, w_pw.astype(BF16), bsz, s_len)
    return _out_proj(ya, yb, True,
                     h, h_batch_major, p, layer, w_out.astype(BF16), post_g, w_gate.astype(BF16),
                     w_ple.astype(BF16), out_batch_major, bsz, s_len, gates=(ga, gb))


def _odd_layer(h, p, layer, out_batch_major, bsz, s_len, rope_cos, rope_sin, pre_g, post_g, w_gate, w_ple,
               w_in, pos_k, pos_v, ck_w1, ck_w2, cv_w1, cv_w2, q_norm, kv_norm, w_uq, w_ukv, w_out):
    assert s_len // SEL_LEN <= NSA_SEL_LANES and s_len // CMP_STRIDE == LANE
    overlap, kvn_tab, cmp_tab = _nsa_constants(s_len)
    w_in_p = _gather_cols(w_in, _ODD_COLS, _ODD_COL_SCALE).astype(BF16)
    q, kraw, vraw, kvn, gl, gn, cq, ckr, gm = _in_proj(h, False, pre_g, w_in_p, _ODD_SEGS, _ODD_DTYPES, False,
                                                bsz, s_len, _ODD_KVN_SEG, kvn_tab)
    eye = jnp.eye(NSA_KV_HEADS, dtype=F32)

    def w1_blockdiag(w1):
        w = w1.reshape(CMP_LEN, NSA_HEAD_DIM, CMP_HIDDEN)
        return jnp.einsum('ldj,yz->lydzj', w, eye).reshape(
            CMP_LEN, NSA_KV_HEADS * NSA_HEAD_DIM, NSA_KV_HEADS * CMP_HIDDEN).astype(BF16)

    pos_lanes = lambda pos: jnp.tile(pos, (1, NSA_KV_HEADS)).reshape(CMP_LEN, 1, NSA_KV_HEADS * NSA_HEAD_DIM)
    kw2 = jnp.pad(ck_w2, ((0, 0), (NSA_D0, LANE - NSA_D0 - NSA_HEAD_DIM))).astype(BF16)
    vw2 = jnp.concatenate([cv_w2, cv_w2], axis=1).astype(BF16)
    kc, vc = _nsa_compress(kraw, vraw, pos_lanes(pos_k), pos_lanes(pos_v), w1_blockdiag(ck_w1), kw2,
                           w1_blockdiag(cv_w1), vw2, cmp_tab, bsz, s_len)
    y_c = _nsa_attn(q, kc, vc, kvn, gl, gn, overlap, bsz, s_len)
    qm, km, vm = _mla_proj(cq, ckr, rope_cos, rope_sin, q_norm, kv_norm,
                           _gather_cols(w_uq, _MLA_QA).astype(BF16), _gather_cols(w_uq, _MLA_QB).astype(BF16),
                           _gather_cols(w_ukv, _MLA_KN).astype(BF16), _gather_cols(w_ukv, _MLA_VV).astype(BF16),
                           bsz, s_len)
    y_d = _mla_attn(qm, km, vm, gm, bsz, s_len)
    return _out_proj(y_c, y_d, False, h, False, p, layer, w_out.astype(BF16), post_g, w_gate.astype(BF16),
                     w_ple.astype(BF16), out_batch_major, bsz, s_len)


def kernel(x, p, positions, pre_norm, post_norm, ple_gate, ple_proj, ev_w_in, s5_lam_re, s5_lam_im, s5_b_re, s5_b_im, s5_c_re, s5_c_im, s5_d, s5_log_step, s5_w_glu, s5_b_glu, cv_w_dw, cv_b_dw, cv_ln_g, cv_ln_b, cv_w_pw, ev_w_out, od_w_in, nsa_pos_k, nsa_pos_v, nsa_ck_w1, nsa_ck_w2, nsa_cv_w1, nsa_cv_w2, mla_q_norm, mla_kv_norm, mla_w_uq, mla_w_ukv, od_w_out):
    bsz, s_len, _ = x.shape
    assert bsz == SUBLANE and s_len % max(TM, NSA_TK, MLA_TK) == 0
    depth = p.shape[0]
    rope_cos, rope_sin = _rope_tables(positions, bsz, s_len)
    h = x
    for i in range(depth):
        j = i // 2
        last = i == depth - 1
        if i % 2 == 0:
            h = _even_layer(h, i == 0, p, i, last, bsz, s_len, pre_norm[i], post_norm[i], ple_gate[i],
                            ple_proj[i], ev_w_in[j], s5_lam_re[j], s5_lam_im[j], s5_b_re[j], s5_b_im[j],
                            s5_c_re[j], s5_c_im[j], s5_d[j], s5_log_step[j], s5_w_glu[j], s5_b_glu[j],
                            cv_w_dw[j], cv_b_dw[j], cv_ln_g[j], cv_ln_b[j], cv_w_pw[j], ev_w_out[j])
        else:
            h = _odd_layer(h, p, i, last, bsz, s_len, rope_cos, rope_sin, pre_norm[i], post_norm[i],
                           ple_gate[i], ple_proj[i], od_w_in[j], nsa_pos_k[j], nsa_pos_v[j], nsa_ck_w1[j],
                           nsa_ck_w2[j], nsa_cv_w1[j], nsa_cv_w2[j], mla_q_norm[j], mla_kv_norm[j],
                           mla_w_uq[j], mla_w_ukv[j], od_w_out[j])
    return h
```

```python
import functools
import math

import jax
import jax.numpy as jnp
import numpy as np
from jax import lax
from jax.experimental import pallas as pl
from jax.experimental.pallas import tpu as pltpu

F32 = jnp.float32
BF16 = jnp.bfloat16

D_MODEL = 1024
DEPTH = 4
PLE_DIM = 256
EPS = 1e-6
NEG = -1e30

S5_WIDTH = 512
S5_GROUP = 16
S5_GROUPS = 32
S5_STATE = 64
S5_HALF_GROUPS = 16
S5_HALF_STATE = S5_HALF_GROUPS * S5_STATE
CONV_WIDTH = 512
CONV_K = 31

NSA_HEADS = 8
NSA_KV_HEADS = 2
NSA_GQA = 4
NSA_HEAD_DIM = 64
NSA_WIDTH = 512
CMP_LEN = 32
CMP_STRIDE = 16
CMP_HIDDEN = 256
SEL_LEN = 64
SEL_TOPN = 8
WINDOW = 512

MLA_HEADS = 8
MLA_NOPE = 64
MLA_ROPE = 32
MLA_V = 64
MLA_Q_RANK = 256
MLA_KV_RANK = 128
MLA_WIDTH = 512
ROPE_BASE = 10000.0

LANE = 128
SUBLANE = 8
VMEM_LIMIT = 56 * 1024 * 1024

TM = 512
S5_TB = 128
CONV_TB = 128
CONV_ROWS = 32
NSA_TQ = 256
NSA_TK = 512
MLA_TQ = 512
MLA_TK = 512
MLA_SUB_ROWS = 256

NSA_SEL_LANES = 32
NSA_D0 = NSA_SEL_LANES
NSA_POS_HI = NSA_D0 + NSA_HEAD_DIM
NSA_POS_LO = NSA_POS_HI + 1
MASKED = -1e30
M_INIT = -1e29
MLA_SCALE_LOG2 = (MLA_NOPE + MLA_ROPE) ** -0.5 * math.log2(math.e)


def _cparams(sem):
    return pltpu.CompilerParams(dimension_semantics=sem, vmem_limit_bytes=VMEM_LIMIT)


def _dot(a, b):
    return jnp.dot(a, b, preferred_element_type=F32)


def _dot_nt(a, b):
    return lax.dot_general(a, b, (((1,), (1,)), ((), ())), preferred_element_type=F32)


def _rms(x, g):
    return x * lax.rsqrt(jnp.mean(x * x, axis=-1, keepdims=True) + EPS) * g


def _sigmoid(x):
    return 1.0 / (1.0 + jnp.exp(-x))


def _silu(x):
    return x * _sigmoid(x)


def _gelu_tanh(x):
    c = math.sqrt(2.0 / math.pi)
    return 0.5 * x * (1.0 + jnp.tanh(c * (x + 0.044715 * (x * x * x))))


def _masked_softmax(s, mask):
    s = jnp.where(mask, s, NEG)
    m = jnp.max(s, axis=-1, keepdims=True)
    e = jnp.where(mask, jnp.exp(s - m), 0.0)
    return e / jnp.maximum(jnp.sum(e, axis=-1, keepdims=True), 1e-30)


def _flash_tile(scores, v, m_ref, l_ref, acc_ref, row_slices, exp=jnp.exp):
    n_rep = scores[0].shape[1] // LANE
    stats = []
    for s, r in zip(scores, row_slices):
        m_old = m_ref[r, :]
        m_new = jnp.maximum(m_old, jnp.max(s, axis=-1, keepdims=True))
        p = exp(s - jnp.concatenate([m_new] * n_rep, axis=1))
        stats.append((p.astype(BF16), exp(m_old - m_new), jnp.sum(p, axis=-1, keepdims=True), m_new))
    pvs = [_dot(p, v) for p, _, _, _ in stats]
    for (_, alpha, row_sum, m_new), pv, r in zip(stats, pvs, row_slices):
        l_ref[r, :] = alpha * l_ref[r, :] + row_sum
        acc_ref[r, :] = alpha * acc_ref[r, :] + pv
        m_ref[r, :] = m_new


def _in_proj_kernel(segs, tab_seg, x_ref, g_ref, w_ref, *refs):
    tab_ref, out_refs = (None, refs) if tab_seg is None else (refs[0], refs[1:])
    xn = _rms(x_ref[...], g_ref[...]).astype(BF16)
    for n, ((c0, c1), o_ref) in enumerate(zip(segs, out_refs)):
        y = _dot(xn, w_ref[:, c0:c1])
        if n == tab_seg:
            y = y + tab_ref[...]
        o_ref[...] = y.astype(o_ref.dtype)


def _in_proj(h, h_batch_major, gain, w, segs, out_dtypes, out_time_major, bsz, s_len, tab_seg=None, tab=None):
    n_t = s_len // TM
    if h_batch_major:
        x_spec = pl.BlockSpec((None, TM, D_MODEL), lambda b, i: (b, i, 0))
    else:
        x_spec = pl.BlockSpec((TM, D_MODEL), lambda b, i: (i, b))
    out_shapes, out_specs = [], []
    for (c0, c1), dt in zip(segs, out_dtypes):
        n = c1 - c0
        if out_time_major:
            out_shapes.append(jax.ShapeDtypeStruct((s_len, bsz * n), dt))
            out_specs.append(pl.BlockSpec((TM, n), lambda b, i: (i, b)))
        else:
            out_shapes.append(jax.ShapeDtypeStruct((bsz, s_len, n), dt))
            out_specs.append(pl.BlockSpec((None, TM, n), lambda b, i: (b, i, 0)))
    in_specs = [x_spec,
                pl.BlockSpec((1, D_MODEL), lambda b, i: (0, 0)),
                pl.BlockSpec(w.shape, lambda b, i: (0, 0))]
    args = [h, gain.reshape(1, D_MODEL), w]
    if tab_seg is not None:
        in_specs.append(pl.BlockSpec((TM, tab.shape[1]), lambda b, i: (i, 0)))
        args.append(tab)
    return pl.pallas_call(
        functools.partial(_in_proj_kernel, tuple(segs), tab_seg),
        grid=(bsz, n_t),
        in_specs=in_specs,
        out_specs=out_specs,
        out_shape=out_shapes,
        compiler_params=_cparams(("parallel", "parallel")),
        name="in_proj",
    )(*args)


def _out_proj_kernel(gated, ya_ref, yb_ref, *refs):
    if gated:
        ga_ref, gb_ref, h_ref, p_ref, wo_ref, gpost_ref, wg_ref, wp_ref, o_ref = refs
        ya = (ya_ref[...] * _silu(ga_ref[...])).astype(BF16)
        yb = (yb_ref[...] * _silu(gb_ref[...])).astype(BF16)
    else:
        h_ref, p_ref, wo_ref, gpost_ref, wg_ref, wp_ref, o_ref = refs
        ya, yb = ya_ref[...], yb_ref[...]
    half = wo_ref.shape[0] // 2
    y = _dot(ya, wo_ref[0:half, :]) + _dot(yb, wo_ref[half:, :])
    h1 = h_ref[...] + _rms(y, gpost_ref[...])
    gate = _sigmoid(_dot(h1.astype(BF16), wg_ref[...]))
    o_ref[...] = h1 + gate * _dot(p_ref[...].astype(BF16), wp_ref[...])


def _out_proj(ya, yb, y_time_major, h, h_batch_major, p, layer, w_out, g_post, w_gate, w_ple,
              out_batch_major, bsz, s_len, gates=None):
    n_t = s_len // TM
    half = w_out.shape[0] // 2
    if y_time_major:
        y_spec = pl.BlockSpec((TM, half), lambda b, i: (i, b))
    else:
        y_spec = pl.BlockSpec((None, TM, half), lambda b, i: (b, i, 0))
    if h_batch_major:
        h_spec = pl.BlockSpec((None, TM, D_MODEL), lambda b, i: (b, i, 0))
    else:
        h_spec = pl.BlockSpec((TM, D_MODEL), lambda b, i: (i, b))
    if out_batch_major:
        o_shape = jax.ShapeDtypeStruct((bsz, s_len, D_MODEL), F32)
        o_spec = pl.BlockSpec((None, TM, D_MODEL), lambda b, i: (b, i, 0))
    else:
        o_shape = jax.ShapeDtypeStruct((s_len, bsz * D_MODEL), F32)
        o_spec = pl.BlockSpec((TM, D_MODEL), lambda b, i: (i, b))
    const = lambda b, i: (0, 0)
    gated = gates is not None
    return pl.pallas_call(
        functools.partial(_out_proj_kernel, gated),
        grid=(bsz, n_t),
        in_specs=[y_spec, y_spec] + ([y_spec, y_spec] if gated else []) + [
                  h_spec,
                  pl.BlockSpec((None, None, TM, PLE_DIM), lambda b, i: (layer, b, i, 0)),
                  pl.BlockSpec(w_out.shape, const),
                  pl.BlockSpec((1, D_MODEL), const),
                  pl.BlockSpec(w_gate.shape, const),
                  pl.BlockSpec(w_ple.shape, const)],
        out_specs=o_spec,
        out_shape=o_shape,
        compiler_params=_cparams(("parallel", "parallel")),
        name="out_proj",
    )(ya, yb, *(gates or ()), h, p, w_out, g_post.reshape(1, D_MODEL), w_gate, w_ple)


def _rows_from_batch_cols(src_ref, slab_ref, width):
    steps = src_ref.shape[0]
    for b in range(SUBLANE):
        for j in range(width // LANE):
            c0 = b * width + j * LANE
            slab_ref[j, pl.ds(b, steps, stride=SUBLANE), :] = src_ref[:, c0:c0 + LANE]


def _batch_cols_from_rows(slab_ref, dst_ref, width):
    steps = dst_ref.shape[0]
    for b in range(SUBLANE):
        for j in range(width // LANE):
            c0 = b * width + j * LANE
            dst_ref[:, c0:c0 + LANE] = slab_ref[j, pl.ds(b, steps, stride=SUBLANE), :].astype(dst_ref.dtype)


def _s5_kernel(u_ref, wb_ref, are_ref, aim_ref, wc_ref, d_ref, wglu_ref, bglu_ref,
               o_ref, buf_ref, st_ref, slab_ref):
    n_steps = u_ref.shape[0]
    rows = n_steps * SUBLANE
    hs = S5_HALF_STATE
    n_slabs = S5_WIDTH // LANE

    @pl.when(pl.program_id(0) == 0)
    def _():
        st_ref[...] = jnp.zeros_like(st_ref)

    _rows_from_batch_cols(u_ref, slab_ref, S5_WIDTH)
    u = jnp.concatenate([slab_ref[j] for j in range(n_slabs)], axis=-1)
    ub = u.astype(BF16)
    ys = []
    for hf in range(2):
        c0 = hf * 2 * hs
        buf_ref[:, c0:c0 + 2 * hs] = _dot(ub[:, hf * 256:(hf + 1) * 256], wb_ref[hf])
        a_re = jnp.broadcast_to(are_ref[hf], (SUBLANE, hs))
        a_im = jnp.broadcast_to(aim_ref[hf], (SUBLANE, hs))

        def step(t, carry, c0=c0, a_re=a_re, a_im=a_im):
            x_re, x_im = carry
            r0 = pl.multiple_of(t * SUBLANE, SUBLANE)
            b_re = buf_ref[pl.ds(r0, SUBLANE), c0:c0 + hs]
            b_im = buf_ref[pl.ds(r0, SUBLANE), c0 + hs:c0 + 2 * hs]
            n_re = a_re * x_re - a_im * x_im + b_re
            n_im = a_re * x_im + a_im * x_re + b_im
            buf_ref[pl.ds(r0, SUBLANE), c0:c0 + hs] = n_re
            buf_ref[pl.ds(r0, SUBLANE), c0 + hs:c0 + 2 * hs] = n_im
            return n_re, n_im

        x_re, x_im = lax.fori_loop(
            0, n_steps, step, (st_ref[:, c0:c0 + hs], st_ref[:, c0 + hs:c0 + 2 * hs]))
        st_ref[:, c0:c0 + hs] = x_re
        st_ref[:, c0 + hs:c0 + 2 * hs] = x_im
        ys.append(_dot(buf_ref[:, c0:c0 + 2 * hs].astype(BF16), wc_ref[hf]))
    y = jnp.concatenate(ys, axis=-1) + d_ref[...] * u
    y = _gelu_tanh(y)
    y = y * _sigmoid(_dot(y.astype(BF16), wglu_ref[...]) + bglu_ref[...])
    for j in range(n_slabs):
        slab_ref[j] = y[:, j * LANE:(j + 1) * LANE]
    _batch_cols_from_rows(slab_ref, o_ref, S5_WIDTH)


def _s5(u, wb, a_re, a_im, wc, d_skip, w_glu, b_glu, bsz, s_len):
    rows = S5_TB * bsz
    n_blk = s_len // S5_TB
    const2 = lambda i: (0, 0)
    const3 = lambda i: (0, 0, 0)
    return pl.pallas_call(
        _s5_kernel,
        grid=(n_blk,),
        in_specs=[pl.BlockSpec((S5_TB, bsz * S5_WIDTH), lambda i: (i, 0)),
                  pl.BlockSpec(wb.shape, const3),
                  pl.BlockSpec(a_re.shape, const3),
                  pl.BlockSpec(a_im.shape, const3),
                  pl.BlockSpec(wc.shape, const3),
                  pl.BlockSpec((1, S5_WIDTH), const2),
                  pl.BlockSpec(w_glu.shape, const2),
                  pl.BlockSpec((1, S5_WIDTH), const2)],
        out_specs=pl.BlockSpec((S5_TB, bsz * S5_WIDTH), lambda i: (i, 0)),
        out_shape=jax.ShapeDtypeStruct((s_len, bsz * S5_WIDTH), BF16),
        scratch_shapes=[pltpu.VMEM((rows, 4 * S5_HALF_STATE), F32),
                        pltpu.VMEM((SUBLANE, 4 * S5_HALF_STATE), F32),
                        pltpu.VMEM((S5_WIDTH // LANE, rows, LANE), F32)],
        compiler_params=_cparams(("arbitrary",)),
        name="s5",
    )(u, wb, a_re, a_im, wc, d_skip.reshape(1, S5_WIDTH), w_glu, b_glu.reshape(1, S5_WIDTH))


def _s5_weights(lam_re, lam_im, b_re, b_im, c_re, c_im, log_step):
    step = jnp.exp(log_step.astype(F32))[:, None]
    lr, li = lam_re.astype(F32), lam_im.astype(F32)
    mag = jnp.exp(lr * step)
    ab_re, ab_im = mag * jnp.cos(li * step), mag * jnp.sin(li * step)
    den = lr * lr + li * li
    nr, ni = ab_re - 1.0, ab_im
    f_re, f_im = (nr * lr + ni * li) / den, (ni * lr - nr * li) / den
    br, bi = b_re.astype(F32), b_im.astype(F32)
    bb_re = f_re[..., None] * br - f_im[..., None] * bi
    bb_im = f_re[..., None] * bi + f_im[..., None] * br
    hg = S5_HALF_GROUPS
    eye = jnp.eye(hg, dtype=F32)

    def b_blockdiag(bb):
        bb = bb.reshape(2, hg, S5_STATE, S5_GROUP)
        return jnp.einsum('hgnc,gk->hgckn', bb, eye).reshape(2, hg * S5_GROUP, hg * S5_STATE)

    def c_blockdiag(cc):
        cc = cc.reshape(2, hg, S5_GROUP, S5_STATE)
        return jnp.einsum('hgcn,gk->hgnkc', cc, eye).reshape(2, hg * S5_STATE, hg * S5_GROUP)

    wb = jnp.concatenate([b_blockdiag(bb_re), b_blockdiag(bb_im)], axis=-1).astype(BF16)
    wc = jnp.concatenate([c_blockdiag(c_re.astype(F32)), -c_blockdiag(c_im.astype(F32))],
                         axis=1).astype(BF16)
    a_re = ab_re.reshape(2, 1, S5_HALF_STATE)
    a_im = ab_im.reshape(2, 1, S5_HALF_STATE)
    return wb, a_re, a_im, wc


def _conv_kernel(v_ref, wdw_ref, bdw_ref, lng_ref, lnb_ref, wpw_ref, o_ref, hbuf_ref, cbuf_ref, slab_ref):
    rows = v_ref.shape[0] * SUBLANE
    halo = (CONV_K - 1) * SUBLANE
    n_slabs = CONV_WIDTH // LANE

    @pl.when(pl.program_id(0) == 0)
    def _():
        hbuf_ref[0:halo, :] = jnp.zeros((halo, CONV_WIDTH), F32)

    @pl.when(pl.program_id(0) > 0)
    def _():
        hbuf_ref[0:halo, :] = hbuf_ref[rows:rows + halo, :]

    _rows_from_batch_cols(v_ref, slab_ref, 2 * CONV_WIDTH)
    for j in range(n_slabs):
        hbuf_ref[halo:halo + rows, j * LANE:(j + 1) * LANE] = slab_ref[j] * _sigmoid(slab_ref[n_slabs + j])

    def chunk(c, carry):
        r0 = pl.multiple_of(c * CONV_ROWS, CONV_ROWS)
        acc = jnp.zeros((CONV_ROWS, CONV_WIDTH), F32)
        for k in range(CONV_K):
            acc = acc + wdw_ref[k:k + 1, :] * hbuf_ref[pl.ds(r0 + k * SUBLANE, CONV_ROWS), :]
        cbuf_ref[pl.ds(r0, CONV_ROWS), :] = acc
        return carry

    lax.fori_loop(0, rows // CONV_ROWS, chunk, 0)
    c = cbuf_ref[...] + bdw_ref[...]
    mu = jnp.mean(c, axis=-1, keepdims=True)
    var = jnp.mean(jnp.square(c - mu), axis=-1, keepdims=True)
    hn = _silu((c - mu) * lax.rsqrt(var + EPS) * lng_ref[...] + lnb_ref[...])
    y = _dot(hn.astype(BF16), wpw_ref[...])
    for j in range(n_slabs):
        slab_ref[j] = y[:, j * LANE:(j + 1) * LANE]
    _batch_cols_from_rows(slab_ref, o_ref, CONV_WIDTH)


def _conv(v, w_dw, b_dw, ln_g, ln_b, w_pw, bsz, s_len):
    rows = CONV_TB * bsz
    n_blk = s_len // CONV_TB
    halo = (CONV_K - 1) * SUBLANE
    const2 = lambda i: (0, 0)
    vec = lambda a: a.reshape(1, CONV_WIDTH)
    return pl.pallas_call(
        _conv_kernel,
        grid=(n_blk,),
        in_specs=[pl.BlockSpec((CONV_TB, bsz * 2 * CONV_WIDTH), lambda i: (i, 0)),
                  pl.BlockSpec((CONV_K, CONV_WIDTH), const2),
                  pl.BlockSpec((1, CONV_WIDTH), const2),
                  pl.BlockSpec((1, CONV_WIDTH), const2),
                  pl.BlockSpec((1, CONV_WIDTH), const2),
                  pl.BlockSpec(w_pw.shape, const2)],
        out_specs=pl.BlockSpec((CONV_TB, bsz * CONV_WIDTH), lambda i: (i, 0)),
        out_shape=jax.ShapeDtypeStruct((s_len, bsz * CONV_WIDTH), BF16),
        scratch_shapes=[pltpu.VMEM((halo + rows, CONV_WIDTH), F32),
                        pltpu.VMEM((rows, CONV_WIDTH), F32),
                        pltpu.VMEM((2 * CONV_WIDTH // LANE, rows, LANE), F32)],
        compiler_params=_cparams(("arbitrary",)),
        name="conv",
    )(v, w_dw, vec(b_dw), vec(ln_g), vec(ln_b), w_pw)


def _nsa_compress_kernel(kraw_ref, vraw_ref, pk_ref, pv_ref, kw1_ref, kw2_ref, vw1_ref, vw2_ref, ktab_ref,
                         ko_ref, vo_ref):
    n_chunks = ko_ref.shape[1]

    def compress(raw_ref, pos_ref, w1_ref, w2_ref, o_ref, tab):
        lo = jnp.zeros((n_chunks, NSA_KV_HEADS * CMP_HIDDEN), F32)
        hi = jnp.zeros((n_chunks, NSA_KV_HEADS * CMP_HIDDEN), F32)
        for l in range(CMP_STRIDE):
            x = raw_ref[pl.ds(l, n_chunks, stride=CMP_STRIDE), :]
            lo = lo + _dot((x + pos_ref[l]).astype(BF16), w1_ref[l])
            hi = hi + _dot((x + pos_ref[CMP_STRIDE + l]).astype(BF16), w1_ref[CMP_STRIDE + l])
        hid = _gelu_tanh(lo + pltpu.roll(hi, n_chunks - 1, 0))
        for y in range(NSA_KV_HEADS):
            out = _dot(hid[:, y * CMP_HIDDEN:(y + 1) * CMP_HIDDEN].astype(BF16), w2_ref[...])
            o_ref[y] = (out if tab is None else out + tab).astype(o_ref.dtype)

    compress(kraw_ref, pk_ref, kw1_ref, kw2_ref, ko_ref, ktab_ref[...])
    compress(vraw_ref, pv_ref, vw1_ref, vw2_ref, vo_ref, None)


def _nsa_compress(kraw, vraw, pos_k, pos_v, kw1, kw2, vw1, vw2, ktab, bsz, s_len):
    n_chunks = s_len // CMP_STRIDE
    const2 = lambda b: (0, 0)
    const3 = lambda b: (0, 0, 0)
    raw_spec = pl.BlockSpec((None, s_len, LANE), lambda b: (b, 0, 0))
    o_shape = jax.ShapeDtypeStruct((bsz, NSA_KV_HEADS, n_chunks, LANE), BF16)
    o_spec = pl.BlockSpec((None, NSA_KV_HEADS, n_chunks, LANE), lambda b: (b, 0, 0, 0))
    return pl.pallas_call(
        _nsa_compress_kernel,
        grid=(bsz,),
        in_specs=[raw_spec, raw_spec,
                  pl.BlockSpec(pos_k.shape, const3),
                  pl.BlockSpec(pos_v.shape, const3),
                  pl.BlockSpec(kw1.shape, const3),
                  pl.BlockSpec(kw2.shape, const2),
                  pl.BlockSpec(vw1.shape, const3),
                  pl.BlockSpec(vw2.shape, const2),
                  pl.BlockSpec(ktab.shape, const2)],
        out_specs=[o_spec, o_spec],
        out_shape=[o_shape, o_shape],
        compiler_params=_cparams(("parallel",)),
        name="nsa_compress",
    )(kraw, vraw, pos_k, pos_v, kw1, kw2, vw1, vw2, ktab)


def _nsa_attn_kernel(s_len, q_ref, kc_ref, vc_ref, ks_ref, vs_ref, kw_ref, vw_ref, gl_ref, gn_ref,
                     ov_ref, o_ref, m_ref, l_ref, acc_ref):
    tq, tk = NSA_TQ, NSA_TK
    y = pl.program_id(1)
    t0 = pl.program_id(2) * tq
    n_cmp = (s_len - CMP_LEN) // CMP_STRIDE + 1
    n_sel_blocks = s_len // SEL_LEN
    n_top = min(SEL_TOPN, n_sel_blocks)
    lane_i = lax.broadcasted_iota(jnp.int32, (tq, LANE), 1)
    tpos = t0 + lax.broadcasted_iota(jnp.int32, (tq, LANE), 0)
    slopes = [jnp.where(y == 0, 2.0 ** -(g + 1), 2.0 ** -(NSA_GQA + g + 1)).astype(F32)
              for g in range(NSA_GQA)]

    heads = range(NSA_GQA)

    def q_heads(extra):
        parts = []
        for g in heads:
            aug = jnp.where(lane_i == NSA_POS_HI, slopes[g] * 256.0,
                            jnp.where(lane_i == NSA_POS_LO, slopes[g], 0.0))
            if extra is not None:
                aug = aug + extra
            parts.append(q_ref[:, g * LANE:(g + 1) * LANE] + aug.astype(BF16))
        return parts

    mask_c = (tpos >= lane_i * CMP_STRIDE + (CMP_LEN - 1)) & (lane_i < n_cmp)
    kc, vc = kc_ref[...], vc_ref[...]
    p_c = [_masked_softmax(_dot_nt(q, kc), mask_c) for q in q_heads(None)]
    o_cmp = [_dot(p.astype(BF16), vc) for p in p_c]

    p_sum = (p_c[0] + p_c[1]) + (p_c[2] + p_c[3])
    p_hi = p_sum.astype(BF16)
    p_lo = (p_sum - p_hi.astype(F32)).astype(BF16)
    p_slc = _dot(p_hi, ov_ref[...]) + _dot(p_lo, ov_ref[...])
    cur = tpos // SEL_LEN
    forced = (lane_i == 0) | (lane_i == cur) | (lane_i == cur - 1)
    causal_blk = lane_i * SEL_LEN <= tpos
    score = jnp.where(forced, p_slc + 1e4, jnp.where(causal_blk, p_slc, -1e4))
    score_t = score.T[0:n_sel_blocks, :]
    j_idx = lax.broadcasted_iota(jnp.int32, score_t.shape, 0)
    rank = jnp.zeros(score_t.shape, F32)
    for blk in range(n_sel_blocks):
        row = score_t[blk:blk + 1, :]
        beats = (row > score_t) | ((row == score_t) & (j_idx > blk))
        rank = rank + jnp.where(beats, 1.0, 0.0)
    bias_t = jnp.where(rank < n_top, 0.0, MASKED)
    sel_bias = jnp.concatenate([bias_t, jnp.zeros((LANE - n_sel_blocks, tq), F32)], axis=0).T

    qs = q_heads(sel_bias)
    m_ref[...] = jnp.full_like(m_ref, M_INIT)
    l_ref[...] = jnp.zeros_like(l_ref)
    acc_ref[...] = jnp.zeros_like(acc_ref)

    def sel_tile(kt, causal_mask):
        k0 = pl.multiple_of(kt * tk, tk)
        k = ks_ref[pl.ds(k0, tk), :]
        v = vs_ref[pl.ds(k0, tk), :]
        scores = [_dot_nt(qs[g], k) for g in heads]
        if causal_mask:
            ok = (k0 + lax.broadcasted_iota(jnp.int32, (tq, tk), 1)
                  <= t0 + lax.broadcasted_iota(jnp.int32, (tq, tk), 0))
            scores = [jnp.where(ok, s, MASKED) for s in scores]
        _flash_tile(scores, v, m_ref, l_ref, acc_ref, [slice(g * tq, (g + 1) * tq) for g in heads])

    n_full = t0 // tk

    def full_tile(kt, carry):
        sel_tile(kt, False)
        return carry

    lax.fori_loop(0, n_full, full_tile, 0)
    sel_tile(n_full, True)

    span = min(WINDOW + tq, s_len)
    ws = pl.multiple_of(jnp.minimum(jnp.maximum(t0 - WINDOW, 0), s_len - span), tq)
    dist_w = ((t0 - ws) + lax.broadcasted_iota(jnp.int32, (tq, span), 0)
              - lax.broadcasted_iota(jnp.int32, (tq, span), 1))
    bias_w = jnp.where((dist_w >= 0) & (dist_w < WINDOW), 0.0, MASKED)
    kw = kw_ref[pl.ds(ws, span), :]
    vw = vw_ref[pl.ds(ws, span), :]
    s_w = [_dot_nt(qs[g], kw) + bias_w for g in heads]
    e_w = [jnp.exp(s - jnp.max(s, axis=-1, keepdims=True)) for s in s_w]
    sum_w = [jnp.maximum(jnp.sum(e, axis=-1, keepdims=True), 1e-30) for e in e_w]
    o_win = [_dot(e.astype(BF16), vw) / d for e, d in zip(e_w, sum_w)]

    gates = _sigmoid(gl_ref[...])
    outs = []
    for g in heads:
        r = slice(g * tq, (g + 1) * tq)
        o_sel = acc_ref[r, :] / jnp.maximum(l_ref[r, :], 1e-30)
        outs.append(gates[:, g:g + 1] * o_cmp[g]
                    + gates[:, NSA_GQA + g:NSA_GQA + g + 1] * o_sel
                    + gates[:, 2 * NSA_GQA + g:2 * NSA_GQA + g + 1] * o_win[g])
    lo = lane_i < NSA_HEAD_DIM
    o = jnp.concatenate([jnp.where(lo, outs[0], outs[1]), jnp.where(lo, outs[2], outs[3])], axis=-1)
    o_ref[...] = (o * _silu(gn_ref[...])).astype(o_ref.dtype)


def _nsa_attn(q, kc, vc, kvn, gl, gn, overlap, bsz, s_len):
    tq = NSA_TQ
    n_chunks = kc.shape[2]
    rows = NSA_GQA * tq
    kv_spec = lambda col: pl.BlockSpec((None, s_len, LANE), lambda b, y, i, col=col: (b, 0, col + y))
    c_spec = pl.BlockSpec((None, None, n_chunks, LANE), lambda b, y, i: (b, y, 0, 0))
    return pl.pallas_call(
        functools.partial(_nsa_attn_kernel, s_len),
        grid=(bsz, NSA_KV_HEADS, s_len // tq),
        in_specs=[pl.BlockSpec((None, tq, NSA_GQA * LANE), lambda b, y, i: (b, i, y)),
                  c_spec, c_spec,
                  kv_spec(0), kv_spec(2), kv_spec(4), kv_spec(6),
                  pl.BlockSpec((None, tq, LANE), lambda b, y, i: (b, i, y)),
                  pl.BlockSpec((None, tq, NSA_GQA * NSA_HEAD_DIM), lambda b, y, i: (b, i, y)),
                  pl.BlockSpec(overlap.shape, lambda b, y, i: (0, 0))],
        out_specs=pl.BlockSpec((None, tq, NSA_GQA * NSA_HEAD_DIM), lambda b, y, i: (b, i, y)),
        out_shape=jax.ShapeDtypeStruct((bsz, s_len, NSA_WIDTH), BF16),
        scratch_shapes=[pltpu.VMEM((rows, LANE), F32)] * 3,
        compiler_params=_cparams(("parallel", "parallel", "parallel")),
        name="nsa_attn",
    )(q, kc, vc, kvn, kvn, kvn, kvn, gl, gn, overlap)


def _rope_table_kernel(pos_ref, freq_ref, cos_ref, sin_ref):
    ang = pos_ref[...].astype(F32) * freq_ref[...]
    lane = lax.broadcasted_iota(jnp.int32, ang.shape, 1)
    half = MLA_ROPE // 2
    first = (lane >= MLA_NOPE) & (lane < MLA_NOPE + half)
    second = (lane >= MLA_NOPE + half) & (lane < MLA_NOPE + MLA_ROPE)
    cos_ref[...] = jnp.where(first | second, jnp.cos(ang), jnp.where(lane < MLA_NOPE, 1.0, 0.0))
    sn = jnp.sin(ang)
    sin_ref[...] = jnp.where(first, -sn, jnp.where(second, sn, 0.0))


def _rope_tables(positions, bsz, s_len):
    half = MLA_ROPE // 2
    freqs = ROPE_BASE ** (-jnp.arange(half, dtype=F32) / half)
    freq_lane = jnp.concatenate([jnp.zeros((MLA_NOPE,), F32), freqs, freqs,
                                 jnp.zeros((LANE - MLA_NOPE - MLA_ROPE,), F32)]).reshape(1, LANE)
    spec = pl.BlockSpec((None, TM, LANE), lambda b, i: (b, i, 0))
    shape = jax.ShapeDtypeStruct((bsz, s_len, LANE), F32)
    return pl.pallas_call(
        _rope_table_kernel,
        grid=(bsz, s_len // TM),
        in_specs=[pl.BlockSpec((None, TM, 1), lambda b, i: (b, i, 0)),
                  pl.BlockSpec((1, LANE), lambda b, i: (0, 0))],
        out_specs=[spec, spec],
        out_shape=[shape, shape],
        compiler_params=_cparams(("parallel", "parallel")),
        name="rope_tables",
    )(positions.reshape(bsz, s_len, 1), freq_lane)


def _mla_proj_kernel(cq_ref, ckr_ref, cos_ref, sin_ref, qn_ref, kvn_ref, wqa_ref, wqb_ref, wk_ref, wv_ref,
                     q_ref, k_ref, v_ref):
    cq = _rms(cq_ref[...], qn_ref[...]).astype(BF16)
    ckv = _rms(ckr_ref[:, 0:MLA_KV_RANK], kvn_ref[...]).astype(BF16)
    cosm = cos_ref[...]
    sinm = sin_ref[...]
    kr = ckr_ref[:, MLA_KV_RANK:MLA_KV_RANK + LANE] * cosm + ckr_ref[:, MLA_KV_RANK + LANE:] * sinm
    v_ref[...] = _dot(ckv, wv_ref[...]).astype(v_ref.dtype)
    for h in range(MLA_HEADS):
        c = slice(h * LANE, (h + 1) * LANE)
        q = _dot(cq, wqa_ref[:, c]) * cosm + _dot(cq, wqb_ref[:, c]) * sinm
        q_ref[:, c] = (q * MLA_SCALE_LOG2).astype(q_ref.dtype)
        k_ref[:, c] = (_dot(ckv, wk_ref[:, c]) + kr).astype(k_ref.dtype)


def _mla_proj(cq, ckr, cosm, sinm, q_norm, kv_norm, wqa, wqb, wk, wv, bsz, s_len):
    const = lambda b, i: (0, 0)
    tok = lambda n: pl.BlockSpec((None, TM, n), lambda b, i: (b, i, 0))
    return pl.pallas_call(
        _mla_proj_kernel,
        grid=(bsz, s_len // TM),
        in_specs=[tok(MLA_Q_RANK), tok(MLA_KV_RANK + 2 * LANE), tok(LANE), tok(LANE),
                  pl.BlockSpec((1, MLA_Q_RANK), const), pl.BlockSpec((1, MLA_KV_RANK), const),
                  pl.BlockSpec(wqa.shape, const), pl.BlockSpec(wqb.shape, const),
                  pl.BlockSpec(wk.shape, const), pl.BlockSpec(wv.shape, const)],
        out_specs=[tok(MLA_HEADS * LANE), tok(MLA_HEADS * LANE), tok(MLA_WIDTH)],
        out_shape=[jax.ShapeDtypeStruct((bsz, s_len, MLA_HEADS * LANE), BF16),
                   jax.ShapeDtypeStruct((bsz, s_len, MLA_HEADS * LANE), BF16),
                   jax.ShapeDtypeStruct((bsz, s_len, MLA_WIDTH), BF16)],
        compiler_params=_cparams(("parallel", "parallel")),
        name="mla_proj",
    )(cq, ckr, cosm, sinm, q_norm.reshape(1, MLA_Q_RANK), kv_norm.reshape(1, MLA_KV_RANK), wqa, wqb, wk, wv)


def _mla_attn_kernel(q_ref, k_ref, v_ref, gm_ref, o_ref, m_ref, l_ref, acc_ref):
    tq, tk = MLA_TQ, MLA_TK
    t0 = pl.program_id(2) * tq
    n_full = t0 // tk
    sub = MLA_SUB_ROWS
    n_sub = tq // sub
    m_ref[...] = jnp.full_like(m_ref, M_INIT)
    l_ref[...] = jnp.zeros_like(l_ref)
    acc_ref[...] = jnp.zeros_like(acc_ref)

    def tile(kt, causal_mask):
        k0 = pl.multiple_of(kt * tk, tk)
        v = v_ref[pl.ds(k0, tk), :]
        scores, row_slices = [], []
        for hh in range(2):
            k = k_ref[pl.ds(k0, tk), hh * LANE:(hh + 1) * LANE]
            for rb in range(n_sub):
                s = _dot_nt(q_ref[rb * sub:(rb + 1) * sub, hh * LANE:(hh + 1) * LANE], k)
                if causal_mask:
                    ok = (k0 + lax.broadcasted_iota(jnp.int32, (sub, tk), 1)
                          <= t0 + rb * sub + lax.broadcasted_iota(jnp.int32, (sub, tk), 0))
                    s = jnp.where(ok, s, MASKED)
                scores.append(s)
                row_slices.append(slice(hh * tq + rb * sub, hh * tq + (rb + 1) * sub))
        _flash_tile(scores, v, m_ref, l_ref, acc_ref, row_slices, exp=jnp.exp2)

    def full_tile(kt, carry):
        tile(kt, False)
        return carry

    lax.fori_loop(0, n_full, full_tile, 0)
    tile(n_full, True)
    o_all = acc_ref[...] / jnp.maximum(l_ref[...], 1e-30)
    lane = lax.broadcasted_iota(jnp.int32, (tq, LANE), 1)
    o = jnp.where(lane < MLA_V, o_all[0:tq], o_all[tq:2 * tq])
    o_ref[...] = (o * _silu(gm_ref[...])).astype(o_ref.dtype)


def _mla_attn(q, k, v, gm, bsz, s_len):
    tq = MLA_TQ
    return pl.pallas_call(
        _mla_attn_kernel,
        grid=(bsz, MLA_HEADS // 2, s_len // tq),
        in_specs=[pl.BlockSpec((None, tq, 2 * LANE), lambda b, hp, i: (b, i, hp)),
                  pl.BlockSpec((None, s_len, 2 * LANE), lambda b, hp, i: (b, 0, hp)),
                  pl.BlockSpec((None, s_len, LANE), lambda b, hp, i: (b, 0, hp)),
                  pl.BlockSpec((None, tq, LANE), lambda b, hp, i: (b, i, hp))],
        out_specs=pl.BlockSpec((None, tq, LANE), lambda b, hp, i: (b, i, hp)),
        out_shape=jax.ShapeDtypeStruct((bsz, s_len, MLA_WIDTH), BF16),
        scratch_shapes=[pltpu.VMEM((2 * tq, LANE), F32)] * 3,
        compiler_params=_cparams(("parallel", "parallel", "parallel")),
        name="mla_attn",
    )(q, k, v, gm)


def _odd_in_layout():
    off = np.cumsum([0, NSA_WIDTH, 6 * NSA_KV_HEADS * NSA_HEAD_DIM, 3 * NSA_HEADS, NSA_WIDTH,
                     MLA_Q_RANK, MLA_KV_RANK, MLA_ROPE, MLA_WIDTH])
    q0, kv0, gl0, gn0, cq0, ckv0, kr0, gm0 = off[:8]
    dh = NSA_HEAD_DIM
    zeros = lambda n: [-1] * n
    cols, segs, start = [], [], 0

    def seg(c):
        nonlocal start
        cols.extend(c)
        segs.append((start, start + len(c)))
        start += len(c)

    lead, tail = zeros(NSA_D0), zeros(LANE - NSA_D0 - dh)
    c = []
    for h in range(NSA_HEADS):
        c += lead + list(range(q0 + h * dh, q0 + (h + 1) * dh)) + tail
    seg(c)
    seg(list(range(kv0, kv0 + NSA_KV_HEADS * dh)))
    seg(list(range(kv0 + NSA_KV_HEADS * dh, kv0 + 2 * NSA_KV_HEADS * dh)))
    c = []
    for slot in range(2, 6):
        for y in range(NSA_KV_HEADS):
            base = kv0 + (slot * NSA_KV_HEADS + y) * dh
            src = list(range(base, base + dh))
            c += (src + src) if slot % 2 == 1 else (lead + src + tail)
    seg(c)
    c = []
    for y in range(NSA_KV_HEADS):
        blk = [gl0 + br * NSA_HEADS + y * NSA_GQA + g for br in range(3) for g in range(NSA_GQA)]
        c += blk + zeros(LANE - len(blk))
    seg(c)
    seg(list(range(gn0, gn0 + NSA_WIDTH)))
    seg(list(range(cq0, cq0 + MLA_Q_RANK)))
    half = MLA_ROPE // 2
    kr = list(range(kr0, kr0 + MLA_ROPE))
    pad_hi = zeros(LANE - MLA_NOPE - MLA_ROPE)
    seg(list(range(ckv0, ckv0 + MLA_KV_RANK))
        + zeros(MLA_NOPE) + kr + pad_hi
        + zeros(MLA_NOPE) + kr[half:] + kr[:half] + pad_hi)
    seg(list(range(gm0, gm0 + MLA_WIDTH)))
    return np.asarray(cols, np.int32), segs


_ODD_COLS, _ODD_SEGS = _odd_in_layout()
_ODD_DTYPES = (BF16, F32, F32, BF16, F32, F32, F32, F32, F32)
_ODD_KVN_SEG = 3
_ODD_COL_SCALE = np.where(np.arange(len(_ODD_COLS)) < _ODD_SEGS[0][1], NSA_HEAD_DIM ** -0.5, 1.0).astype(np.float32)


def _gather_cols(w, cols, col_scale=None):
    cols = np.asarray(cols)
    same_run = ((cols[1:] < 0) & (cols[:-1] < 0)) | ((cols[1:] >= 0) & (cols[:-1] >= 0) & (np.diff(cols) == 1))
    breaks = np.flatnonzero(~same_run) + 1
    parts = []
    for run in np.split(cols, breaks):
        if run[0] < 0:
            parts.append(jnp.zeros((w.shape[0], len(run)), w.dtype))
        else:
            parts.append(w[:, int(run[0]):int(run[0]) + len(run)])
    g = jnp.concatenate(parts, axis=1)
    return g if col_scale is None else g * jnp.asarray(col_scale)[None, :]


def _mla_weight_layout():
    dq = MLA_NOPE + MLA_ROPE
    half = MLA_ROPE // 2
    qa, qb, kn, vv = [], [], [], []
    for h in range(MLA_HEADS):
        nope = list(range(h * dq, h * dq + MLA_NOPE))
        rope = list(range(h * dq + MLA_NOPE, (h + 1) * dq))
        pad = [-1] * (LANE - dq)
        qa += nope + rope + pad
        qb += [-1] * MLA_NOPE + rope[half:] + rope[:half] + pad
        kv0 = h * (MLA_NOPE + MLA_V)
        kn += list(range(kv0, kv0 + MLA_NOPE)) + [-1] * (LANE - MLA_NOPE)
        vv += list(range(kv0 + MLA_NOPE, kv0 + MLA_NOPE + MLA_V))
    return tuple(np.asarray(c, np.int32) for c in (qa, qb, kn, vv))


_MLA_QA, _MLA_QB, _MLA_KN, _MLA_VV = _mla_weight_layout()


def _nsa_constants(s_len):
    n_cmp = (s_len - CMP_LEN) // CMP_STRIDE + 1
    n_sel = s_len // SEL_LEN
    cs = np.arange(LANE)[:, None] * CMP_STRIDE
    sb = np.arange(LANE)[None]
    overlap = ((cs < (sb + 1) * SEL_LEN) & (cs + CMP_LEN > sb * SEL_LEN)
               & (np.arange(LANE)[:, None] < n_cmp) & (sb < n_sel)).astype(np.float32)

    def pos_lanes(pos):
        tab = np.zeros((len(pos), LANE), np.float32)
        tab[:, NSA_POS_HI] = pos // 256
        tab[:, NSA_POS_LO] = pos % 256
        return tab

    t = np.arange(s_len)
    tab_win = pos_lanes(t)
    tab_sel = pos_lanes(t)
    tab_sel[t, t // SEL_LEN] = 1.0
    zero = np.zeros((s_len, LANE), np.float32)
    kvn_tab = np.concatenate([tab_sel, tab_sel, zero, zero, tab_win, tab_win, zero, zero], axis=1)
    cmp_tab = pos_lanes(np.arange(LANE) * CMP_STRIDE + (CMP_LEN - 1))
    return jnp.asarray(overlap, BF16), jnp.asarray(kvn_tab), jnp.asarray(cmp_tab)


_EVEN_SEGS = ((0, 512), (512, 1024), (1024, 2048), (2048, 2560))


def _even_layer(h, h_batch_major, p, layer, out_batch_major, bsz, s_len, pre_g, post_g, w_gate, w_ple,
                w_in, lam_re, lam_im, b_re, b_im, c_re, c_im, d_skip, log_step, w_glu, b_glu,
                w_dw, b_dw, ln_g, ln_b, w_pw, w_out):
    ua, ga, vb, gb = _in_proj(h, h_batch_major, pre_g, w_in.astype(BF16), _EVEN_SEGS, (F32,) * 4, True,
                              bsz, s_len)
    wb, a_re, a_im, wc = _s5_weights(lam_re, lam_im, b_re, b_im, c_re, c_im, log_step)
    ya = _s5(ua, wb, a_re, a_im, wc, d_skip, w_glu.astype(BF16), b_glu, bsz, s_len)
    yb = _conv(vb, w_dw, b_dw, ln_g, ln_b, w_pw.astype(BF16), bsz, s_len)
    return _out_proj(ya, yb, True,
                     h, h_batch_major, p, layer, w_out.astype(BF16), post_g, w_gate.astype(BF16),
                     w_ple.astype(BF16), out_batch_major, bsz, s_len, gates=(ga, gb))


def _odd_layer(h, p, layer, out_batch_major, bsz, s_len, rope_cos, rope_sin, pre_g, post_g, w_gate, w_ple,
               w_in, pos_k, pos_v, ck_w1, ck_w2, cv_w1, cv_w2, q_norm, kv_norm, w_uq, w_ukv, w_out):
    assert s_len // SEL_LEN <= NSA_SEL_LANES and s_len // CMP_STRIDE == LANE
    overlap, kvn_tab, cmp_tab = _nsa_constants(s_len)
    w_in_p = _gather_cols(w_in, _ODD_COLS, _ODD_COL_SCALE).astype(BF16)
    q, kraw, vraw, kvn, gl, gn, cq, ckr, gm = _in_proj(h, False, pre_g, w_in_p, _ODD_SEGS, _ODD_DTYPES, False,
                                                bsz, s_len, _ODD_KVN_SEG, kvn_tab)
    eye = jnp.eye(NSA_KV_HEADS, dtype=F32)

    def w1_blockdiag(w1):
        w = w1.reshape(CMP_LEN, NSA_HEAD_DIM, CMP_HIDDEN)
        return jnp.einsum('ldj,yz->lydzj', w, eye).reshape(
            CMP_LEN, NSA_KV_HEADS * NSA_HEAD_DIM, NSA_KV_HEADS * CMP_HIDDEN).astype(BF16)

    pos_lanes = lambda pos: jnp.tile(pos, (1, NSA_KV_HEADS)).reshape(CMP_LEN, 1, NSA_KV_HEADS * NSA_HEAD_DIM)
    kw2 = jnp.pad(ck_w2, ((0, 0), (NSA_D0, LANE - NSA_D0 - NSA_HEAD_DIM))).astype(BF16)
    vw2 = jnp.concatenate([cv_w2, cv_w2], axis=1).astype(BF16)
    kc, vc = _nsa_compress(kraw, vraw, pos_lanes(pos_k), pos_lanes(pos_v), w1_blockdiag(ck_w1), kw2,
                           w1_blockdiag(cv_w1), vw2, cmp_tab, bsz, s_len)
    y_c = _nsa_attn(q, kc, vc, kvn, gl, gn, overlap, bsz, s_len)
    qm, km, vm = _mla_proj(cq, ckr, rope_cos, rope_sin, q_norm, kv_norm,
                           _gather_cols(w_uq, _MLA_QA).astype(BF16), _gather_cols(w_uq, _MLA_QB).astype(BF16),
                           _gather_cols(w_ukv, _MLA_KN).astype(BF16), _gather_cols(w_ukv, _MLA_VV).astype(BF16),
                           bsz, s_len)
    y_d = _mla_attn(qm, km, vm, gm, bsz, s_len)
    return _out_proj(y_c, y_d, False, h, False, p, layer, w_out.astype(BF16), post_g, w_gate.astype(BF16),
                     w_ple.astype(BF16), out_batch_major, bsz, s_len)


def kernel(x, p, positions, pre_norm, post_norm, ple_gate, ple_proj, ev_w_in, s5_lam_re, s5_lam_im, s5_b_re, s5_b_im, s5_c_re, s5_c_im, s5_d, s5_log_step, s5_w_glu, s5_b_glu, cv_w_dw, cv_b_dw, cv_ln_g, cv_ln_b, cv_w_pw, ev_w_out, od_w_in, nsa_pos_k, nsa_pos_v, nsa_ck_w1, nsa_ck_w2, nsa_cv_w1, nsa_cv_w2, mla_q_norm, mla_kv_norm, mla_w_uq, mla_w_ukv, od_w_out):
    bsz, s_len, _ = x.shape
    assert bsz == SUBLANE and s_len % max(TM, NSA_TK, MLA_TK) == 0
    depth = p.shape[0]
    rope_cos, rope_sin = _rope_tables(positions, bsz, s_len)
    h = x
    for i in range(depth):
        j = i // 2
        last = i == depth - 1
        if i % 2 == 0:
            h = _even_layer(h, i == 0, p, i, last, bsz, s_len, pre_norm[i], post_norm[i], ple_gate[i],
                            ple_proj[i], ev_w_in[j], s5_lam_re[j], s5_lam_im[j], s5_b_re[j], s5_b_im[j],
                            s5_c_re[j], s5_c_im[j], s5_d[j], s5_log_step[j], s5_w_glu[j], s5_b_glu[j],
                            cv_w_dw[j], cv_b_dw[j], cv_ln_g[j], cv_ln_b[j], cv_w_pw[j], ev_w_out[j])
        else:
            h = _odd_layer(h, p, i, last, bsz, s_len, rope_cos, rope_sin, pre_norm[i], post_norm[i],
                           ple_gate[i], ple_proj[i], od_w_in[j], nsa_pos_k[j], nsa_pos_v[j], nsa_ck_w1[j],
                           nsa_ck_w2[j], nsa_cv_w1[j], nsa_cv_w2[j], mla_q_norm[j], mla_kv_norm[j],
                           mla_w_uq[j], mla_w_ukv[j], od_w_out[j])
    return h
```
